```python
import jax, jax.numpy as jnp
from jax import lax
import numpy as np

D_MODEL = 1024
BATCH = 4
SEQ = 4096
DEPTH = 4
DEC_BATCH = 32
DEC_SEQ = 8
PAST_LEN = 8192
PAGE_SIZE = 128

N_A = DEPTH // 2
N_B = DEPTH - N_A
CONV_WIDTH = 31
N_HEADS = 16
HEAD_DIM = D_MODEL // N_HEADS
WINDOWS = (128, 512, 2048)
DILATIONS = (1, 4, 16)
N_GROUPS = len(WINDOWS)
BLK = 128
D_FF = (8 * D_MODEL + 3 * 256 - 1) // (3 * 256) * 256
ROPE_THETA = 10000.0
EPS = 1e-6
SCALE = HEAD_DIM ** -0.5

kernel_name = 'yoco_conformer_dilated_swa_decoder'


def rms_norm(x, g):
    xf = x.astype(jnp.float32)
    y = xf * lax.rsqrt(jnp.mean(xf * xf, axis=-1, keepdims=True) + EPS)
    return (y * g.astype(jnp.float32)).astype(x.dtype)


def layer_norm(x, g, b):
    xf = x.astype(jnp.float32)
    mu = jnp.mean(xf, axis=-1, keepdims=True)
    xc = xf - mu
    y = xc * lax.rsqrt(jnp.mean(xc * xc, axis=-1, keepdims=True) + EPS)
    return (y * g.astype(jnp.float32) + b.astype(jnp.float32)).astype(x.dtype)


def rotary(x, pos):
    half = HEAD_DIM // 2
    inv = ROPE_THETA ** (-jnp.arange(half, dtype=jnp.float32) / half)
    ang = pos.astype(jnp.float32)[:, None] * inv[None, :]
    shp = (1, pos.shape[0]) + (1,) * (x.ndim - 3) + (half,)
    cos = jnp.cos(ang).reshape(shp)
    sin = jnp.sin(ang).reshape(shp)
    xf = x.astype(jnp.float32)
    x1, x2 = xf[..., :half], xf[..., half:]
    return jnp.concatenate([x1 * cos - x2 * sin, x2 * cos + x1 * sin], axis=-1).astype(x.dtype)


def conv_module(x, prev, norm_g, w_in, b_in, w_dw, b_dw, ln_g, ln_b, w_out, b_out):
    u = rms_norm(x, norm_g)
    a = u @ w_in + b_in
    glu = a[..., :D_MODEL] * jax.nn.sigmoid(a[..., D_MODEL:])
    ext = jnp.concatenate([prev.astype(glu.dtype), glu], axis=1)
    c = lax.conv_general_dilated(ext, w_dw[:, None, :].astype(ext.dtype), window_strides=(1,),
                                 padding='VALID', dimension_numbers=('NWC', 'WIO', 'NWC'),
                                 feature_group_count=D_MODEL) + b_dw
    h = jax.nn.silu(layer_norm(c, ln_g, ln_b))
    return h @ w_out + b_out, ext[:, -(CONV_WIDTH - 1):]


def swiglu(x, g, w_gate_up, w_down):
    gu = rms_norm(x, g) @ w_gate_up
    return (jax.nn.silu(gu[..., :D_FF]) * gu[..., D_FF:]) @ w_down


def shared_kv(s, pos, kv_norm, w_kv, k_norm):
    B, S, _ = s.shape
    kv = (rms_norm(s, kv_norm) @ w_kv).reshape(B, S, N_GROUPS, 2, N_HEADS, HEAD_DIM)
    k = rotary(rms_norm(kv[:, :, :, 0], k_norm[None, None, :, None, :]), pos)
    v = kv[:, :, :, 1]
    return k, v


def queries(x, pos, b_norm, w_q, q_norm):
    B, S, _ = x.shape
    q = (rms_norm(x, b_norm) @ w_q).reshape(B, S, N_GROUPS, N_HEADS, HEAD_DIM)
    return rotary(rms_norm(q, q_norm[None, None, :, None, :]), pos)


def to_blocks(x, r):
    B, S, H, hd = x.shape
    span = r * BLK
    Sp = -(-S // span) * span
    L = Sp // r
    x = jnp.pad(x, ((0, 0), (0, Sp - S), (0, 0), (0, 0)))
    x = x.reshape(B, L, r, H, hd).transpose(0, 2, 1, 3, 4)
    return x.reshape(B, r, L // BLK, BLK, H, hd)


def band_keys(k, v, r):
    def with_prev(xb):
        prev = jnp.concatenate([jnp.zeros_like(xb[:, :, :1]), xb[:, :, :-1]], axis=2)
        return jnp.concatenate([prev, xb], axis=3)
    return with_prev(to_blocks(k, r)), with_prev(to_blocks(v, r))


def band_attention(q, kb2, vb2, r, n_strided):
    B, S, H, hd = q.shape
    nb = kb2.shape[2]
    L = nb * BLK
    Sp = L * r
    qb = to_blocks(q, r)
    s = jnp.einsum('brnqhd,brnkhd->brnhqk', qb.astype(jnp.float32), kb2.astype(jnp.float32)) * SCALE
    qi = jnp.arange(BLK)[:, None]
    ki = jnp.arange(2 * BLK)[None, :]
    dist = BLK + qi - ki
    band = (dist >= 0) & (dist <= n_strided)
    after_start = (jnp.arange(nb)[:, None, None] > 0) | (ki[None] >= BLK)
    mask = band[None] & after_start
    s = jnp.where(mask[None, None, :, None], s, -jnp.inf)
    lse = jax.nn.logsumexp(s, axis=-1)
    p = jnp.exp(s - lse[..., None])
    o = jnp.einsum('brnhqk,brnkhd->brnqhd', p, vb2.astype(jnp.float32))
    o = o.reshape(B, r, L, H, hd).transpose(0, 2, 1, 3, 4).reshape(B, Sp, H, hd)[:, :S]
    lse = lse.transpose(0, 1, 2, 4, 3).reshape(B, r, L, H).transpose(0, 2, 1, 3).reshape(B, Sp, H)[:, :S]
    return o, lse


def gather_keys(buf, k_new, v_new, r, n_strided):
    Lb = buf.shape[1]
    T = k_new.shape[1]
    kall = jnp.concatenate([buf[:, :, 0].astype(k_new.dtype), k_new], axis=1)
    vall = jnp.concatenate([buf[:, :, 1].astype(v_new.dtype), v_new], axis=1)
    idx = Lb + jnp.arange(T)[:, None] - r * jnp.arange(n_strided + 1)[None, :]
    valid = idx >= 0
    idx = jnp.maximum(idx, 0)
    kg = kall[:, idx]
    vg = vall[:, idx]
    new_buf = jnp.stack([kall[:, -Lb:], vall[:, -Lb:]], axis=2)
    return (kg, vg, valid), new_buf


def gathered_attention(q, kg, vg, valid):
    s = jnp.einsum('bthd,btjhd->bhtj', q.astype(jnp.float32), kg.astype(jnp.float32)) * SCALE
    s = jnp.where(valid[None, None], s, -jnp.inf)
    lse = jax.nn.logsumexp(s, axis=-1)
    p = jnp.exp(s - lse[..., None])
    o = jnp.einsum('bhtj,btjhd->bthd', p, vg.astype(jnp.float32))
    return o, lse.transpose(0, 2, 1)


def merge_groups(outs, lses, w_o, dtype):
    o = jnp.stack(outs, axis=0)
    w = jax.nn.softmax(jnp.stack(lses, axis=0), axis=0)
    o = jnp.sum(w[..., None] * o, axis=0)
    B, S = o.shape[:2]
    return o.reshape(B, S, N_HEADS * HEAD_DIM).astype(dtype) @ w_o


def run_trunk(x, pos, conv_prev, kv_bufs, p):
    conv_states = []
    keyside, new_bufs = [], []
    for layer in range(DEPTH):
        if layer < N_A:
            h, st = conv_module(x, conv_prev[layer], p['a_norm'][layer], p['a_w_in'][layer], p['a_b_in'][layer],
                                p['a_w_dw'][layer], p['a_b_dw'][layer], p['a_ln_g'][layer], p['a_ln_b'][layer],
                                p['a_w_out'][layer], p['a_b_out'][layer])
            x = x + h
            conv_states.append(st)
        else:
            if layer == N_A:
                k, v = shared_kv(x, pos, p['kv_norm'], p['w_kv'], p['k_norm'])
                for g in range(N_GROUPS):
                    r = DILATIONS[g]
                    if kv_bufs is None:
                        keyside.append(band_keys(k[:, :, g], v[:, :, g], r))
                        lg = min(WINDOWS[g], x.shape[1])
                        new_bufs.append(jnp.stack([k[:, -lg:, g], v[:, -lg:, g]], axis=2))
                    else:
                        ks, nbuf = gather_keys(kv_bufs[g], k[:, :, g], v[:, :, g], r, WINDOWS[g] // r)
                        keyside.append(ks)
                        new_bufs.append(nbuf)
            j = layer - N_A
            q = queries(x, pos, p['b_norm'][j], p['w_q'][j], p['q_norm'][j])
            outs, lses = [], []
            for g in range(N_GROUPS):
                r = DILATIONS[g]
                if kv_bufs is None:
                    o, l = band_attention(q[:, :, g], keyside[g][0], keyside[g][1], r, WINDOWS[g] // r)
                else:
                    o, l = gathered_attention(q[:, :, g], *keyside[g])
                outs.append(o)
                lses.append(l)
            x = x + merge_groups(outs, lses, p['w_o'][j], x.dtype)
        x = x + swiglu(x, p['ffn_norm'][layer], p['w_gate_up'][layer], p['w_down'][layer])
    return x, jnp.stack(conv_states, axis=0), new_bufs


def setup_inputs(seed: int = 0) -> dict:
    key = jax.random.key(seed)
    ks = jax.random.split(key, 32)
    f32 = jnp.float32

    def nrm(k, shape, scale):
        return scale * jax.random.normal(k, shape, f32)

    GH = N_GROUPS * N_HEADS * HEAD_DIM
    return {
        'x_prompt': nrm(ks[0], (BATCH, SEQ, D_MODEL), 1.0),
        'x_sample': nrm(ks[1], (DEC_BATCH, DEC_SEQ, D_MODEL), 1.0),
        'cache_conv': nrm(ks[2], (N_A, DEC_BATCH, CONV_WIDTH - 1, D_MODEL), 0.5),
        'cache_kv_w128': nrm(ks[3], (DEC_BATCH, min(WINDOWS[0], PAST_LEN), 2, N_HEADS, HEAD_DIM), 1.0),
        'cache_kv_w512': nrm(ks[4], (DEC_BATCH, min(WINDOWS[1], PAST_LEN), 2, N_HEADS, HEAD_DIM), 1.0),
        'cache_kv_w2048': nrm(ks[5], (DEC_BATCH, min(WINDOWS[2], PAST_LEN), 2, N_HEADS, HEAD_DIM), 1.0),
        'a_norm': 1.0 + nrm(ks[6], (N_A, D_MODEL), 0.05),
        'a_w_in': nrm(ks[7], (N_A, D_MODEL, 2 * D_MODEL), D_MODEL ** -0.5),
        'a_b_in': nrm(ks[8], (N_A, 2 * D_MODEL), 0.02),
        'a_w_dw': nrm(ks[9], (N_A, CONV_WIDTH, D_MODEL), CONV_WIDTH ** -0.5),
        'a_b_dw': nrm(ks[10], (N_A, D_MODEL), 0.02),
        'a_ln_g': 1.0 + nrm(ks[11], (N_A, D_MODEL), 0.05),
        'a_ln_b': nrm(ks[12], (N_A, D_MODEL), 0.02),
        'a_w_out': nrm(ks[13], (N_A, D_MODEL, D_MODEL), D_MODEL ** -0.5),
        'a_b_out': nrm(ks[14], (N_A, D_MODEL), 0.02),
        'kv_norm': 1.0 + nrm(ks[15], (D_MODEL,), 0.05),
        'w_kv': nrm(ks[16], (D_MODEL, 2 * GH), D_MODEL ** -0.5),
        'k_norm': 1.0 + nrm(ks[17], (N_GROUPS, HEAD_DIM), 0.05),
        'b_norm': 1.0 + nrm(ks[18], (N_B, D_MODEL), 0.05),
        'w_q': nrm(ks[19], (N_B, D_MODEL, GH), D_MODEL ** -0.5),
        'q_norm': 1.0 + nrm(ks[20], (N_B, N_GROUPS, HEAD_DIM), 0.05),
        'w_o': nrm(ks[21], (N_B, N_HEADS * HEAD_DIM, D_MODEL), (N_HEADS * HEAD_DIM) ** -0.5),
        'ffn_norm': 1.0 + nrm(ks[22], (DEPTH, D_MODEL), 0.05),
        'w_gate_up': nrm(ks[23], (DEPTH, D_MODEL, 2 * D_FF), D_MODEL ** -0.5),
        'w_down': nrm(ks[24], (DEPTH, D_FF, D_MODEL), D_FF ** -0.5),
    }


def reference(x_prompt, x_sample, cache_conv, cache_kv_w128, cache_kv_w512, cache_kv_w2048,
              a_norm, a_w_in, a_b_in, a_w_dw, a_b_dw, a_ln_g, a_ln_b, a_w_out, a_b_out,
              kv_norm, w_kv, k_norm, b_norm, w_q, q_norm, w_o, ffn_norm, w_gate_up, w_down):
    p = dict(a_norm=a_norm, a_w_in=a_w_in, a_b_in=a_b_in, a_w_dw=a_w_dw, a_b_dw=a_b_dw,
             a_ln_g=a_ln_g, a_ln_b=a_ln_b, a_w_out=a_w_out, a_b_out=a_b_out,
             kv_norm=kv_norm, w_kv=w_kv, k_norm=k_norm, b_norm=b_norm, w_q=w_q, q_norm=q_norm,
             w_o=w_o, ffn_norm=ffn_norm, w_gate_up=w_gate_up, w_down=w_down)
    B, S, _ = x_prompt.shape
    T = x_sample.shape[1]
    pos_prompt = jnp.arange(S, dtype=jnp.int32)
    pos_sample = PAST_LEN + jnp.arange(T, dtype=jnp.int32)
    conv_zero = jnp.zeros((N_A, B, CONV_WIDTH - 1, D_MODEL), x_prompt.dtype)
    y_prompt, conv_p, bufs_p = run_trunk(x_prompt, pos_prompt, conv_zero, None, p)
    y_sample, conv_s, bufs_s = run_trunk(x_sample, pos_sample, cache_conv,
                                         (cache_kv_w128, cache_kv_w512, cache_kv_w2048), p)
    return (y_prompt, y_sample, conv_p, conv_s, bufs_p[0], bufs_p[1], bufs_p[2], bufs_s[0], bufs_s[1], bufs_s[2])
```

```python
import functools

import jax
import jax.numpy as jnp
from jax import lax
from jax.experimental import pallas as pl
from jax.experimental.pallas import tpu as pltpu

D_MODEL = 1024
DEPTH = 4
PAST_LEN = 8192
N_A = DEPTH // 2
N_B = DEPTH - N_A
CONV_WIDTH = 31
CONV_HALO = 32
N_HEADS = 16
HEAD_DIM = D_MODEL // N_HEADS
WINDOWS = (128, 512, 2048)
DILATIONS = (1, 4, 16)
N_GROUPS = len(WINDOWS)
BLK = 128
D_FF = (8 * D_MODEL + 3 * 256 - 1) // (3 * 256) * 256
ROPE_THETA = 10000.0
EPS = 1e-6
SCALE = HEAD_DIM ** -0.5
LANES = 128
VMEM_LIMIT = 56 * 1024 * 1024

F32 = jnp.float32
BF16 = jnp.bfloat16


def _params(*sem):
    return pltpu.CompilerParams(dimension_semantics=sem, vmem_limit_bytes=VMEM_LIMIT)


def _rms(x, g):
    return x * lax.rsqrt(jnp.mean(x * x, axis=-1, keepdims=True) + EPS) * g


def _sigmoid(x):
    return 1.0 / (1.0 + jnp.exp(-x))


def _conv_kernel(x_ref, prev_ref, ng_ref, win_ref, bin_ref, wdw_ref, bdw_ref, lng_ref, lnb_ref,
                 wout_ref, bout_ref, y_ref, st_ref, ext_ref, *, ts):
    s = pl.program_id(1)
    bb = x_ref.shape[0]

    @pl.when(s == 0)
    def _():
        ext_ref[:, 0:CONV_HALO, :] = prev_ref[...]

    @pl.when(s > 0)
    def _():
        ext_ref[:, 0:CONV_HALO, :] = ext_ref[:, ts:ts + CONV_HALO, :]

    x = x_ref[...].reshape(bb * ts, D_MODEL)
    u = _rms(x, ng_ref[...]).astype(BF16)
    a = jnp.dot(u, win_ref[...], preferred_element_type=F32) + bin_ref[...]
    glu = a[:, :D_MODEL] * _sigmoid(a[:, D_MODEL:])
    ext_ref[:, CONV_HALO:CONV_HALO + ts, :] = glu.reshape(bb, ts, D_MODEL)

    off = CONV_HALO - (CONV_WIDTH - 1)
    c = ext_ref[:, off:off + ts, :] * wdw_ref[0:1, :]
    for j in range(1, CONV_WIDTH):
        c = c + ext_ref[:, off + j:off + j + ts, :] * wdw_ref[j:j + 1, :]
    c = c.reshape(bb * ts, D_MODEL) + bdw_ref[...]
    mu = jnp.mean(c, axis=-1, keepdims=True)
    cc = c - mu
    yn = cc * lax.rsqrt(jnp.mean(cc * cc, axis=-1, keepdims=True) + EPS) * lng_ref[...] + lnb_ref[...]
    h = (yn * _sigmoid(yn)).astype(BF16)
    out = x + jnp.dot(h, wout_ref[...], preferred_element_type=F32) + bout_ref[...]
    y_ref[...] = out.reshape(bb, ts, D_MODEL)
    st_ref[...] = ext_ref[:, ts:ts + CONV_HALO, :]


def _conv_layer(x, prev32, ng, win, bin_, wdw, bdw, lng, lnb, wout, bout, *, bb, ts):
    B, S, _ = x.shape
    full = lambda shape: pl.BlockSpec(shape, lambda b, s: (0,) * len(shape))
    return pl.pallas_call(
        functools.partial(_conv_kernel, ts=ts),
        grid=(B // bb, S // ts),
        in_specs=[
            pl.BlockSpec((bb, ts, D_MODEL), lambda b, s: (b, s, 0)),
            pl.BlockSpec((bb, CONV_HALO, D_MODEL), lambda b, s: (b, 0, 0)),
            full((1, D_MODEL)), full((D_MODEL, 2 * D_MODEL)), full((1, 2 * D_MODEL)),
            full((CONV_WIDTH, D_MODEL)), full((1, D_MODEL)), full((1, D_MODEL)), full((1, D_MODEL)),
            full((D_MODEL, D_MODEL)), full((1, D_MODEL)),
        ],
        out_specs=[
            pl.BlockSpec((bb, ts, D_MODEL), lambda b, s: (b, s, 0)),
            pl.BlockSpec((bb, CONV_HALO, D_MODEL), lambda b, s: (b, 0, 0)),
        ],
        out_shape=[jax.ShapeDtypeStruct((B, S, D_MODEL), F32),
                   jax.ShapeDtypeStruct((B, CONV_HALO, D_MODEL), F32)],
        scratch_shapes=[pltpu.VMEM((bb, CONV_HALO + ts, D_MODEL), F32)],
        compiler_params=_params("parallel", "arbitrary"),
        name="conv_module",
    )(x, prev32, ng, win, bin_, wdw, bdw, lng, lnb, wout, bout)


FF_SPLIT = 2
FF_TILE = D_FF // FF_SPLIT


def _ffn_kernel(x_ref, g_ref, wg_ref, wu_ref, wd_ref, y_ref, xn_ref, acc_ref):
    j = pl.program_id(1)

    @pl.when(j == 0)
    def _():
        xn_ref[...] = _rms(x_ref[...], g_ref[...]).astype(BF16)
        acc_ref[...] = jnp.zeros_like(acc_ref)

    xn = xn_ref[...]
    gate = jnp.dot(xn, wg_ref[...], preferred_element_type=F32)
    up = jnp.dot(xn, wu_ref[...], preferred_element_type=F32)
    h = (gate * _sigmoid(gate) * up).astype(BF16)
    acc_ref[...] += jnp.dot(h, wd_ref[...], preferred_element_type=F32)

    @pl.when(j == FF_SPLIT - 1)
    def _():
        y_ref[...] = x_ref[...] + acc_ref[...]


def _ffn_layer(x, g, wgu, wd, *, tm):
    M = x.shape[0]
    return pl.pallas_call(
        _ffn_kernel,
        grid=(M // tm, FF_SPLIT),
        in_specs=[
            pl.BlockSpec((tm, D_MODEL), lambda i, j: (i, 0)),
            pl.BlockSpec((1, D_MODEL), lambda i, j: (0, 0)),
            pl.BlockSpec((D_MODEL, FF_TILE), lambda i, j: (0, j)),
            pl.BlockSpec((D_MODEL, FF_TILE), lambda i, j: (0, FF_SPLIT + j)),
            pl.BlockSpec((FF_TILE, D_MODEL), lambda i, j: (j, 0)),
        ],
        out_specs=pl.BlockSpec((tm, D_MODEL), lambda i, j: (i, 0)),
        out_shape=jax.ShapeDtypeStruct((M, D_MODEL), F32),
        scratch_shapes=[pltpu.VMEM((tm, D_MODEL), BF16), pltpu.VMEM((tm, D_MODEL), F32)],
        compiler_params=_params("parallel", "arbitrary"),
        name="swiglu_ffn",
    )(x, g, wgu, wgu, wd)


def _head_norm_rotary(y, gain, cos, sin):
    lane = lax.broadcasted_iota(jnp.int32, y.shape, 1)
    first = lane < HEAD_DIM
    sq = y * y
    s_all = jnp.sum(sq, axis=-1, keepdims=True)
    s_a = jnp.sum(jnp.where(first, sq, 0.0), axis=-1, keepdims=True)
    r_a = lax.rsqrt(s_a * (1.0 / HEAD_DIM) + EPS)
    r_b = lax.rsqrt((s_all - s_a) * (1.0 / HEAD_DIM) + EPS)
    yn = y * jnp.where(first, r_a, r_b) * gain
    lo_half = (lane % HEAD_DIM) < (HEAD_DIM // 2)
    partner = jnp.where(lo_half, pltpu.roll(yn, LANES - HEAD_DIM // 2, 1), pltpu.roll(yn, HEAD_DIM // 2, 1))
    return yn * cos + partner * sin


def _proj_kernel(x_ref, g_ref, w_ref, gain_ref, cos_ref, sin_ref, o_ref, xn_ref, *, plain_every):
    c = pl.program_id(1)

    @pl.when(c == 0)
    def _():
        xn_ref[...] = _rms(x_ref[...], g_ref[...]).astype(BF16)

    y = jnp.dot(xn_ref[...], w_ref[...], preferred_element_type=F32)

    def normed():
        cos = cos_ref[...]
        sin = sin_ref[...]
        for p in range(N_HEADS // 2):
            sl = slice(p * LANES, (p + 1) * LANES)
            o_ref[:, sl] = _head_norm_rotary(y[:, sl], gain_ref[0, :, sl], cos, sin)

    if plain_every == 0:
        normed()
    else:
        pl.when(c % plain_every == 0)(normed)

        @pl.when(c % plain_every != 0)
        def _():
            o_ref[...] = y


def _proj(x, g, w, gain, cos, sin, *, tm, plain_every):
    M = x.shape[0]
    n_chunks = w.shape[1] // D_MODEL
    n_pos_tiles = cos.shape[0] // tm
    gdiv = max(plain_every, 1)
    return pl.pallas_call(
        functools.partial(_proj_kernel, plain_every=plain_every),
        grid=(M // tm, n_chunks),
        in_specs=[
            pl.BlockSpec((tm, D_MODEL), lambda i, c: (i, 0)),
            pl.BlockSpec((1, D_MODEL), lambda i, c: (0, 0)),
            pl.BlockSpec((D_MODEL, D_MODEL), lambda i, c: (0, c)),
            pl.BlockSpec((1, 1, D_MODEL), lambda i, c: (c // gdiv, 0, 0)),
            pl.BlockSpec((tm, LANES), lambda i, c: (i % n_pos_tiles, 0)),
            pl.BlockSpec((tm, LANES), lambda i, c: (i % n_pos_tiles, 0)),
        ],
        out_specs=pl.BlockSpec((tm, D_MODEL), lambda i, c: (i, c)),
        out_shape=jax.ShapeDtypeStruct((M, n_chunks * D_MODEL), F32),
        scratch_shapes=[pltpu.VMEM((tm, D_MODEL), BF16)],
        compiler_params=_params("parallel", "arbitrary"),
        name="norm_proj_rotary",
    )(x, g, w, gain, cos, sin)


def _rope_tables(pos):
    half = HEAD_DIM // 2
    inv = ROPE_THETA ** (-jnp.arange(half, dtype=F32) / half)
    ang = pos.astype(F32)[:, None] * inv[None, :]
    cos, sin = jnp.cos(ang), jnp.sin(ang)
    reps = LANES // HEAD_DIM
    return jnp.tile(cos, (1, 2 * reps)), jnp.tile(jnp.concatenate([-sin, sin], axis=1), (1, reps))


def _softmax_pv(s, v):
    m = jnp.max(s, axis=-1, keepdims=True)
    p = jnp.exp(s - m)
    l = jnp.sum(p, axis=-1, keepdims=True)
    o = jnp.dot(p.astype(BF16), v, preferred_element_type=F32) * (1.0 / l)
    return o, m + jnp.log(l)


def _qk(q, k):
    return lax.dot_general(q, k, (((1,), (1,)), ((), ())), preferred_element_type=F32)


def _band_attn_kernel(q_ref, k_ref, v_ref, o_ref, lse_ref, kf_ref, vf_ref, *, tq):
    n = pl.program_id(2)

    @pl.when(n == 0)
    def _():
        kf_ref[0:BLK, :] = jnp.zeros((BLK, D_MODEL), BF16)
        vf_ref[0:BLK, :] = jnp.zeros((BLK, D_MODEL), BF16)

    @pl.when(n > 0)
    def _():
        kf_ref[0:BLK, :] = kf_ref[tq:tq + BLK, :]
        vf_ref[0:BLK, :] = vf_ref[tq:tq + BLK, :]

    kf_ref[BLK:BLK + tq, :] = k_ref[0].astype(BF16)
    vf_ref[BLK:BLK + tq, :] = v_ref[0].astype(BF16)

    qi = lax.broadcasted_iota(jnp.int32, (BLK, 2 * BLK), 0)
    ki = lax.broadcasted_iota(jnp.int32, (BLK, 2 * BLK), 1)
    dist = BLK + qi - ki
    band = (dist >= 0) & (dist <= BLK)
    lane = lax.broadcasted_iota(jnp.int32, (BLK, LANES), 1)

    def block(i, carry):
        r0 = pl.multiple_of(i * BLK, BLK)
        qb = q_ref[0, pl.ds(r0, BLK), :].astype(BF16)
        kb = kf_ref[pl.ds(r0, 2 * BLK), :]
        vb = vf_ref[pl.ds(r0, 2 * BLK), :]
        mask = band & ((ki >= BLK) | (n + i > 0))
        lse_tile = jnp.zeros((BLK, LANES), F32)
        for p in range(N_HEADS // 2):
            outs = []
            for h in (2 * p, 2 * p + 1):
                sl = slice(h * HEAD_DIM, (h + 1) * HEAD_DIM)
                s = jnp.where(mask, _qk(qb[:, sl], kb[:, sl]), -jnp.inf)
                o, lse = _softmax_pv(s, vb[:, sl])
                outs.append(o)
                lse_tile = jnp.where(lane == h, lse, lse_tile)
            o_ref[0, pl.ds(r0, BLK), p * LANES:(p + 1) * LANES] = jnp.concatenate(outs, axis=-1)
        lse_ref[0, pl.ds(r0, BLK), :] = lse_tile
        return carry

    lax.fori_loop(0, tq // BLK, block, 0)


def _band_attention(q, kv, g, *, B, S):
    r = DILATIONS[g]
    L = S // r
    tq = min(L, 4 * BLK)
    qv = q.reshape(B, L, r * N_GROUPS * D_MODEL)
    kvv = kv.reshape(B, L, r * 2 * N_GROUPS * D_MODEL)
    o, lse = pl.pallas_call(
        functools.partial(_band_attn_kernel, tq=tq),
        grid=(B, r, L // tq),
        in_specs=[
            pl.BlockSpec((1, tq, D_MODEL), lambda b, c, n: (b, n, c * N_GROUPS + g)),
            pl.BlockSpec((1, tq, D_MODEL), lambda b, c, n: (b, n, c * 2 * N_GROUPS + 2 * g)),
            pl.BlockSpec((1, tq, D_MODEL), lambda b, c, n: (b, n, c * 2 * N_GROUPS + 2 * g + 1)),
        ],
        out_specs=[
            pl.BlockSpec((1, tq, D_MODEL), lambda b, c, n: (b, n, c)),
            pl.BlockSpec((1, tq, LANES), lambda b, c, n: (b, n, c)),
        ],
        out_shape=[jax.ShapeDtypeStruct((B, L, r * D_MODEL), F32),
                   jax.ShapeDtypeStruct((B, L, r * LANES), F32)],
        scratch_shapes=[pltpu.VMEM((BLK + tq, D_MODEL), BF16), pltpu.VMEM((BLK + tq, D_MODEL), BF16)],
        compiler_params=_params("parallel", "parallel", "arbitrary"),
        name=f"band_attention_g{g}",
    )(qv, kvv, kvv)
    return o.reshape(B * S, D_MODEL), lse.reshape(B * S, LANES)


NEW_PAD = LANES


def _cache_attn_kernel(q_ref, kvn_ref, c_ref, o_ref, lse_ref, kv_ref, *, lb, r, t):
    kv_ref[0:NEW_PAD, :] = jnp.zeros((NEW_PAD, 2 * D_MODEL), BF16)
    kv_ref[0:t, :] = kvn_ref[0].astype(BF16)
    kv_ref[NEW_PAD:NEW_PAD + lb, :] = c_ref[0].astype(BF16)
    q = q_ref[0].astype(BF16)

    ti = lax.broadcasted_iota(jnp.int32, (t, NEW_PAD + lb), 0)
    rho = lax.broadcasted_iota(jnp.int32, (t, NEW_PAD + lb), 1)
    idx = jnp.where(rho < NEW_PAD, lb + rho, rho - NEW_PAD)
    d = lb + ti - idx
    mask = ((rho < t) | (rho >= NEW_PAD)) & (d >= 0) & (d <= WINDOWS[DILATIONS.index(r)]) & ((d & (r - 1)) == 0)
    lane = lax.broadcasted_iota(jnp.int32, (t, LANES), 1)

    lse_tile = jnp.zeros((t, LANES), F32)
    for p in range(N_HEADS // 2):
        outs = []
        for h in (2 * p, 2 * p + 1):
            ks = slice(h * HEAD_DIM, (h + 1) * HEAD_DIM)
            vs = slice(D_MODEL + h * HEAD_DIM, D_MODEL + (h + 1) * HEAD_DIM)
            s = jnp.where(mask, _qk(q[:, ks], kv_ref[:, ks]), -jnp.inf)
            o, lse = _softmax_pv(s, kv_ref[:, vs])
            outs.append(o)
            lse_tile = jnp.where(lane == h, lse, lse_tile)
        o_ref[0, :, p * LANES:(p + 1) * LANES] = jnp.concatenate(outs, axis=-1)
    lse_ref[0] = lse_tile


def _cache_attention(q, kvn, cache, g, *, B, T):
    lb = cache.shape[1]
    r = DILATIONS[g]
    o, lse = pl.pallas_call(
        functools.partial(_cache_attn_kernel, lb=lb, r=r, t=T),
        grid=(B,),
        in_specs=[
            pl.BlockSpec((1, T, D_MODEL), lambda b: (b, 0, g)),
            pl.BlockSpec((1, T, 2 * D_MODEL), lambda b: (b, 0, g)),
            pl.BlockSpec((1, lb, 2 * D_MODEL), lambda b: (b, 0, 0)),
        ],
        out_specs=[
            pl.BlockSpec((1, T, D_MODEL), lambda b: (b, 0, 0)),
            pl.BlockSpec((1, T, LANES), lambda b: (b, 0, 0)),
        ],
        out_shape=[jax.ShapeDtypeStruct((B, T, D_MODEL), F32), jax.ShapeDtypeStruct((B, T, LANES), F32)],
        scratch_shapes=[pltpu.VMEM((NEW_PAD + lb, 2 * D_MODEL), BF16)],
        compiler_params=_params("parallel"),
        name=f"cache_attention_g{g}",
    )(q, kvn, cache)
    return o.reshape(B * T, D_MODEL), lse.reshape(B * T, LANES)


def _cache_roll_kernel(c_ref, kvn_ref, o_ref, *, lb, t):
    o_ref[0, 0:lb - t, :] = c_ref[0, t:lb, :]
    o_ref[0, lb - t:lb, :] = kvn_ref[0]


def _cache_roll(cache, kvn, g, *, T):
    B, lb, C = cache.shape
    cw = 4 * LANES
    nper = C // cw
    return pl.pallas_call(
        functools.partial(_cache_roll_kernel, lb=lb, t=T),
        grid=(B, nper),
        in_specs=[
            pl.BlockSpec((1, lb, cw), lambda b, j: (b, 0, j)),
            pl.BlockSpec((1, T, cw), lambda b, j: (b, 0, g * nper + j)),
        ],
        out_specs=pl.BlockSpec((1, lb, cw), lambda b, j: (b, 0, j)),
        out_shape=jax.ShapeDtypeStruct((B, lb, C), F32),
        compiler_params=_params("parallel", "parallel"),
        name=f"cache_roll_g{g}",
    )(cache, kvn)


def _merge_kernel(x_ref, o0_ref, o1_ref, o2_ref, l0_ref, l1_ref, l2_ref, wo_ref, y_ref):
    l0, l1, l2 = l0_ref[...], l1_ref[...], l2_ref[...]
    m = jnp.maximum(jnp.maximum(l0, l1), l2)
    e0, e1, e2 = jnp.exp(l0 - m), jnp.exp(l1 - m), jnp.exp(l2 - m)
    inv = 1.0 / (e0 + e1 + e2)
    w0, w1, w2 = e0 * inv, e1 * inv, e2 * inv
    first = lax.broadcasted_iota(jnp.int32, (x_ref.shape[0], LANES), 1) < HEAD_DIM
    cols = []
    for p in range(N_HEADS // 2):
        sl = slice(p * LANES, (p + 1) * LANES)
        acc = None
        for w, o_ref in ((w0, o0_ref), (w1, o1_ref), (w2, o2_ref)):
            wp = jnp.where(first, w[:, 2 * p:2 * p + 1], w[:, 2 * p + 1:2 * p + 2])
            term = wp * o_ref[:, sl]
            acc = term if acc is None else acc + term
        cols.append(acc.astype(BF16))
    merged = jnp.concatenate(cols, axis=-1)
    y_ref[...] = x_ref[...] + jnp.dot(merged, wo_ref[...], preferred_element_type=F32)


def _merge_layer(x, outs, lses, wo, *, tm):
    M = x.shape[0]
    row = lambda w: pl.BlockSpec((tm, w), lambda i: (i, 0))
    return pl.pallas_call(
        _merge_kernel,
        grid=(M // tm,),
        in_specs=[row(D_MODEL)] * 4 + [row(LANES)] * 3 + [pl.BlockSpec((D_MODEL, D_MODEL), lambda i: (0, 0))],
        out_specs=row(D_MODEL),
        out_shape=jax.ShapeDtypeStruct((M, D_MODEL), F32),
        compiler_params=_params("parallel"),
        name="merge_groups_wo",
    )(x, *outs, *lses, wo)


def _trunk(x, pos, conv_prev, kv_caches, w, *, bb, ts, tm):
    B, S, _ = x.shape
    M = B * S
    cos, sin = _rope_tables(pos)
    if S < tm:
        cos, sin = jnp.tile(cos, (tm // S, 1)), jnp.tile(sin, (tm // S, 1))
    prev32 = jnp.pad(conv_prev, ((0, 0), (0, 0), (CONV_HALO - (CONV_WIDTH - 1), 0), (0, 0)))

    conv_states = []
    new_bufs = []
    kv = None
    for layer in range(DEPTH):
        if layer < N_A:
            x, st = _conv_layer(x, prev32[layer], w['a_norm'][layer], w['a_w_in'][layer], w['a_b_in'][layer],
                                w['a_w_dw'][layer], w['a_b_dw'][layer], w['a_ln_g'][layer], w['a_ln_b'][layer],
                                w['a_w_out'][layer], w['a_b_out'][layer], bb=bb, ts=ts)
            conv_states.append(st[:, CONV_HALO - (CONV_WIDTH - 1):])
            xf = x.reshape(M, D_MODEL)
        else:
            j = layer - N_A
            if layer == N_A:
                kv = _proj(xf, w['kv_norm'], w['w_kv'], w['k_gain'], cos, sin, tm=tm, plain_every=2)
                for g in range(N_GROUPS):
                    lo, hi = 2 * g * D_MODEL, 2 * (g + 1) * D_MODEL
                    if kv_caches is None:
                        lg = min(WINDOWS[g], S)
                        buf = kv.reshape(B, S, -1)[:, S - lg:, lo:hi]
                    else:
                        cache = kv_caches[g].reshape(B, -1, 2 * D_MODEL)
                        lg = cache.shape[1]
                        buf = _cache_roll(cache, kv.reshape(B, S, -1), g, T=S)
                    new_bufs.append(buf.reshape(B, lg, 2, N_HEADS, HEAD_DIM))
            q = _proj(xf, w['b_norm'][j], w['w_q'][j], w['q_gain'][j], cos, sin, tm=tm, plain_every=0)
            outs, lses = [], []
            for g in range(N_GROUPS):
                if kv_caches is None:
                    o, l = _band_attention(q, kv, g, B=B, S=S)
                else:
                    o, l = _cache_attention(q.reshape(B, S, -1), kv.reshape(B, S, -1),
                                            kv_caches[g].reshape(B, -1, 2 * D_MODEL), g, B=B, T=S)
                outs.append(o)
                lses.append(l)
            xf = _merge_layer(xf, outs, lses, w['w_o'][j], tm=tm)
        xf = _ffn_layer(xf, w['ffn_norm'][layer], w['w_gate_up'][layer], w['w_down'][layer], tm=tm)
        x = xf.reshape(B, S, D_MODEL)
    return x, jnp.stack(conv_states, axis=0), new_bufs


def kernel(x_prompt, x_sample, cache_conv, cache_kv_w128, cache_kv_w512, cache_kv_w2048,
           a_norm, a_w_in, a_b_in, a_w_dw, a_b_dw, a_ln_g, a_ln_b, a_w_out, a_b_out,
           kv_norm, w_kv, k_norm, b_norm, w_q, q_norm, w_o, ffn_norm, w_gate_up, w_down):
    B, S, _ = x_prompt.shape
    Bd, T, _ = x_sample.shape
    row = lambda a: a.reshape(a.shape[:-1] + (1, a.shape[-1]))
    w = dict(
        a_norm=row(a_norm), a_w_in=a_w_in.astype(BF16), a_b_in=row(a_b_in), a_w_dw=a_w_dw, a_b_dw=row(a_b_dw),
        a_ln_g=row(a_ln_g), a_ln_b=row(a_ln_b), a_w_out=a_w_out.astype(BF16), a_b_out=row(a_b_out),
        kv_norm=row(kv_norm), w_kv=w_kv.astype(BF16), b_norm=row(b_norm), w_q=w_q.astype(BF16),
        w_o=w_o.astype(BF16), ffn_norm=row(ffn_norm), w_gate_up=w_gate_up.astype(BF16), w_down=w_down.astype(BF16),
        k_gain=row(jnp.tile(k_norm, (1, N_HEADS))),
        q_gain=row(jnp.tile(q_norm, (1, 1, N_HEADS)) * SCALE),
    )
    pos_prompt = jnp.arange(S, dtype=jnp.int32)
    pos_sample = PAST_LEN + jnp.arange(T, dtype=jnp.int32)
    conv_zero = jnp.zeros((N_A, B, CONV_WIDTH - 1, D_MODEL), x_prompt.dtype)
    y_p, conv_p, bufs_p = _trunk(x_prompt, pos_prompt, conv_zero, None, w, bb=1, ts=512, tm=512)
    y_s, conv_s, bufs_s = _trunk(x_sample, pos_sample, cache_conv,
                                 (cache_kv_w128, cache_kv_w512, cache_kv_w2048), w, bb=Bd, ts=T, tm=Bd * T)
    return (y_p, y_s, conv_p, conv_s, bufs_p[0], bufs_p[1], bufs_p[2], bufs_s[0], bufs_s[1], bufs_s[2])
```

```python
import functools

import jax
import jax.numpy as jnp
from jax import lax
from jax.experimental import pallas as pl
from jax.experimental.pallas import tpu as pltpu

D_MODEL = 1024
DEPTH = 4
PAST_LEN = 8192
N_A = DEPTH // 2
N_B = DEPTH - N_A
CONV_WIDTH = 31
CONV_HALO = 32
SUBLANES = 8
LANES = 128
N_HEADS = 16
HEAD_DIM = D_MODEL // N_HEADS
WINDOWS = (128, 512, 2048)
DILATIONS = (1, 4, 16)
N_GROUPS = len(WINDOWS)
BLK = 128
D_FF = (8 * D_MODEL + 3 * 256 - 1) // (3 * 256) * 256
ROPE_THETA = 10000.0
EPS = 1e-6
SCALE = HEAD_DIM ** -0.5
MXU_N = 256
VMEM_LIMIT = 56 * 1024 * 1024

F32 = jnp.float32
BF16 = jnp.bfloat16


def _params(*sem):
    return pltpu.CompilerParams(dimension_semantics=sem, vmem_limit_bytes=VMEM_LIMIT)


def _rms(x, g):
    return x * lax.rsqrt(jnp.mean(x * x, axis=-1, keepdims=True) + EPS) * g


def _sigmoid(x):
    return 1.0 / (1.0 + jnp.exp(-x))


def _conv_kernel(x_ref, prev_ref, ng_ref, win_ref, bin_ref, wdw_ref, bdw_ref, lng_ref, lnb_ref,
                 wout_ref, bout_ref, y_ref, st_ref, ext_ref, *, ts):
    s = pl.program_id(1)
    bb = x_ref.shape[0]

    @pl.when(s == 0)
    def _():
        ext_ref[:, 0:CONV_HALO, :] = prev_ref[...]

    @pl.when(s > 0)
    def _():
        ext_ref[:, 0:CONV_HALO, :] = ext_ref[:, ts:ts + CONV_HALO, :]

    x = x_ref[...].reshape(bb * ts, D_MODEL)
    u = _rms(x, ng_ref[...]).astype(BF16)
    a = jnp.dot(u, win_ref[...], preferred_element_type=F32) + bin_ref[...]
    glu = a[:, :D_MODEL] * _sigmoid(a[:, D_MODEL:])
    ext_ref[:, CONV_HALO:CONV_HALO + ts, :] = glu.reshape(bb, ts, D_MODEL)

    off = CONV_HALO - (CONV_WIDTH - 1)
    c = None
    for b in range(SUBLANES):
        rows = ts if b == 0 else ts + SUBLANES
        yb = None
        for a_ in range((off + CONV_WIDTH - 1) // SUBLANES + 1):
            j = SUBLANES * a_ + b - off
            if 0 <= j < CONV_WIDTH:
                term = ext_ref[:, SUBLANES * a_:SUBLANES * a_ + rows, :] * wdw_ref[j:j + 1, :]
                yb = term if yb is None else yb + term
        yb = yb[:, b:b + ts, :]
        c = yb if c is None else c + yb
    c = c.reshape(bb * ts, D_MODEL) + bdw_ref[...]
    mu = jnp.mean(c, axis=-1, keepdims=True)
    cc = c - mu
    yn = cc * lax.rsqrt(jnp.mean(cc * cc, axis=-1, keepdims=True) + EPS) * lng_ref[...] + lnb_ref[...]
    h = (yn * _sigmoid(yn)).astype(BF16)
    out = x + jnp.dot(h, wout_ref[...], preferred_element_type=F32) + bout_ref[...]
    y_ref[...] = out.reshape(bb, ts, D_MODEL)
    st_ref[...] = ext_ref[:, ts:ts + CONV_HALO, :]


def _conv_layer(x, prev32, ng, win, bin_, wdw, bdw, lng, lnb, wout, bout, *, bb, ts):
    B, S, _ = x.shape
    full = lambda shape: pl.BlockSpec(shape, lambda b, s: (0,) * len(shape))
    return pl.pallas_call(
        functools.partial(_conv_kernel, ts=ts),
        grid=(B // bb, S // ts),
        in_specs=[
            pl.BlockSpec((bb, ts, D_MODEL), lambda b, s: (b, s, 0)),
            pl.BlockSpec((bb, CONV_HALO, D_MODEL), lambda b, s: (b, 0, 0)),
            full((1, D_MODEL)), full((D_MODEL, 2 * D_MODEL)), full((1, 2 * D_MODEL)),
            full((CONV_WIDTH, D_MODEL)), full((1, D_MODEL)), full((1, D_MODEL)), full((1, D_MODEL)),
            full((D_MODEL, D_MODEL)), full((1, D_MODEL)),
        ],
        out_specs=[
            pl.BlockSpec((bb, ts, D_MODEL), lambda b, s: (b, s, 0)),
            pl.BlockSpec((bb, CONV_HALO, D_MODEL), lambda b, s: (b, 0, 0)),
        ],
        out_shape=[jax.ShapeDtypeStruct((B, S, D_MODEL), F32),
                   jax.ShapeDtypeStruct((B, CONV_HALO, D_MODEL), F32)],
        scratch_shapes=[pltpu.VMEM((bb, CONV_HALO + ts, D_MODEL), F32)],
        compiler_params=_params("parallel", "arbitrary"),
        name="conv_module",
    )(x, prev32, ng, win, bin_, wdw, bdw, lng, lnb, wout, bout)


FF_SPLIT = 2
FF_TILE = D_FF // FF_SPLIT


def _ffn_kernel(x_ref, g_ref, wg_ref, wu_ref, wd_ref, y_ref, xn_ref, acc_ref):
    j = pl.program_id(1)

    @pl.when(j == 0)
    def _():
        xn_ref[...] = _rms(x_ref[...], g_ref[...]).astype(BF16)
        acc_ref[...] = jnp.zeros_like(acc_ref)

    xn = xn_ref[...]
    gate = jnp.dot(xn, wg_ref[...], preferred_element_type=F32)
    up = jnp.dot(xn, wu_ref[...], preferred_element_type=F32)
    h = (gate * _sigmoid(gate) * up).astype(BF16)
    acc_ref[...] += jnp.dot(h, wd_ref[...], preferred_element_type=F32)

    @pl.when(j == FF_SPLIT - 1)
    def _():
        y_ref[...] = x_ref[...] + acc_ref[...]


def _ffn_layer(x, g, wgu, wd, *, tm):
    M = x.shape[0]
    return pl.pallas_call(
        _ffn_kernel,
        grid=(M // tm, FF_SPLIT),
        in_specs=[
            pl.BlockSpec((tm, D_MODEL), lambda i, j: (i, 0)),
            pl.BlockSpec((1, D_MODEL), lambda i, j: (0, 0)),
            pl.BlockSpec((D_MODEL, FF_TILE), lambda i, j: (0, j)),
            pl.BlockSpec((D_MODEL, FF_TILE), lambda i, j: (0, FF_SPLIT + j)),
            pl.BlockSpec((FF_TILE, D_MODEL), lambda i, j: (j, 0)),
        ],
        out_specs=pl.BlockSpec((tm, D_MODEL), lambda i, j: (i, 0)),
        out_shape=jax.ShapeDtypeStruct((M, D_MODEL), F32),
        scratch_shapes=[pltpu.VMEM((tm, D_MODEL), BF16), pltpu.VMEM((tm, D_MODEL), F32)],
        compiler_params=_params("parallel", "arbitrary"),
        name="swiglu_ffn",
    )(x, g, wgu, wgu, wd)


def _normed_slab(y, gain, cos, sin, e):
    ssq = jnp.dot((y * y).astype(BF16), e, preferred_element_type=F32)
    yn = y * lax.rsqrt(ssq * (1.0 / HEAD_DIM) + EPS) * gain
    lane = lax.broadcasted_iota(jnp.int32, (y.shape[0], LANES), 1)
    lo_half = (lane % HEAD_DIM) < (HEAD_DIM // 2)
    outs = []
    for k in range(MXU_N // LANES):
        z = yn[:, k * LANES:(k + 1) * LANES]
        partner = jnp.where(lo_half, pltpu.roll(z, LANES - HEAD_DIM // 2, 1), pltpu.roll(z, HEAD_DIM // 2, 1))
        outs.append(z * cos + partner * sin)
    return jnp.concatenate(outs, axis=-1)


def _proj_kernel(*refs, tm, n_chunks, plain_every, tails, n_s):
    x_ref, g_ref, w_ref, gain_ref, cos_ref, sin_ref, e_ref, ob_ref = refs[:8]
    tail_refs = refs[8:8 + len(tails)]
    xn_ref = refs[-1]
    i = pl.program_id(0)
    c = pl.program_id(1)
    s = i % n_s

    @pl.when(c == 0)
    def _():
        xn_ref[...] = _rms(x_ref[...], g_ref[...]).astype(BF16)

    def chunk(normed, tail_ref, tail_len, col0):
        xn = xn_ref[...]
        for k in range(D_MODEL // MXU_N):
            sl = slice(k * MXU_N, (k + 1) * MXU_N)
            y = jnp.dot(xn, w_ref[:, sl], preferred_element_type=F32)
            if normed:
                y = _normed_slab(y, gain_ref[0, :, sl], cos_ref[...], sin_ref[...], e_ref[...])
            ob_ref[:, sl] = y.astype(BF16)
            if tail_ref is not None:
                tsl = slice(col0 + k * MXU_N, col0 + (k + 1) * MXU_N)
                if tail_len >= tm:
                    @pl.when(s >= n_s - tail_len // tm)
                    def _():
                        tail_ref[0, :, tsl] = y
                else:
                    @pl.when(s == n_s - 1)
                    def _():
                        tail_ref[0, :, tsl] = y[tm - tail_len:, :]

    if plain_every == 0:
        chunk(True, None, 0, 0)
    else:
        for cs in range(n_chunks):
            g, part = divmod(cs, plain_every)
            pl.when(c == cs)(functools.partial(chunk, part == 0, tail_refs[g], tails[g], part * D_MODEL))


def _proj(x, g, w, gain, cos, sin, e, *, tm, plain_every, seq=None, tails=()):
    M = x.shape[0]
    n_chunks = w.shape[1] // D_MODEL
    n_pos_tiles = cos.shape[0] // tm
    gdiv = max(plain_every, 1)
    seq = seq or M
    n_s = seq // tm
    out_shape = [jax.ShapeDtypeStruct((M, n_chunks * D_MODEL), BF16)]
    out_specs = [pl.BlockSpec((tm, D_MODEL), lambda i, c: (i, c))]
    for lg in tails:
        out_shape.append(jax.ShapeDtypeStruct((M // seq, lg, 2 * D_MODEL), F32))
        if lg >= tm:
            first = n_s - lg // tm
            out_specs.append(pl.BlockSpec((1, tm, 2 * D_MODEL),
                                          lambda i, c, first=first: (i // n_s, jnp.maximum(i % n_s - first, 0), 0)))
        else:
            out_specs.append(pl.BlockSpec((1, lg, 2 * D_MODEL), lambda i, c: (i // n_s, 0, 0)))
    return pl.pallas_call(
        functools.partial(_proj_kernel, tm=tm, n_chunks=n_chunks, plain_every=plain_every, tails=tuple(tails), n_s=n_s),
        grid=(M // tm, n_chunks),
        in_specs=[
            pl.BlockSpec((tm, D_MODEL), lambda i, c: (i, 0)),
            pl.BlockSpec((1, D_MODEL), lambda i, c: (0, 0)),
            pl.BlockSpec((D_MODEL, D_MODEL), lambda i, c: (0, c)),
            pl.BlockSpec((1, 1, D_MODEL), lambda i, c: (c // gdiv, 0, 0)),
            pl.BlockSpec((tm, LANES), lambda i, c: (i % n_pos_tiles, 0)),
            pl.BlockSpec((tm, LANES), lambda i, c: (i % n_pos_tiles, 0)),
            pl.BlockSpec((MXU_N, MXU_N), lambda i, c: (0, 0)),
        ],
        out_specs=out_specs,
        out_shape=out_shape,
        scratch_shapes=[pltpu.VMEM((tm, D_MODEL), BF16)],
        compiler_params=_params("arbitrary", "arbitrary"),
        name="norm_proj_rotary",
    )(x, g, w, gain, cos, sin, e)


def _rope_tables(pos):
    half = HEAD_DIM // 2
    inv = ROPE_THETA ** (-jnp.arange(half, dtype=F32) / half)
    ang = pos.astype(F32)[:, None] * inv[None, :]
    cos, sin = jnp.cos(ang), jnp.sin(ang)
    reps = LANES // HEAD_DIM
    return jnp.tile(cos, (1, 2 * reps)), jnp.tile(jnp.concatenate([-sin, sin], axis=1), (1, reps))


def _qk(q, k):
    return lax.dot_general(q, k, (((1,), (1,)), ((), ())), preferred_element_type=F32)


def _head_masks(rows):
    first = lax.broadcasted_iota(jnp.int32, (rows, LANES), 1) < HEAD_DIM
    return first


def _band_attn_kernel(q_ref, k_ref, v_ref, o_ref, lse_ref, kf_ref, vf_ref, s_ref, p_ref, *, tq):
    n = pl.program_id(2)

    @pl.when(n == 0)
    def _():
        kf_ref[0:BLK, :] = jnp.zeros((BLK, D_MODEL), BF16)
        vf_ref[0:BLK, :] = jnp.zeros((BLK, D_MODEL), BF16)

    @pl.when(n > 0)
    def _():
        kf_ref[0:BLK, :] = kf_ref[tq:tq + BLK, :]
        vf_ref[0:BLK, :] = vf_ref[tq:tq + BLK, :]

    kf_ref[BLK:BLK + tq, :] = k_ref[0]
    vf_ref[BLK:BLK + tq, :] = v_ref[0]

    qi = lax.broadcasted_iota(jnp.int32, (BLK, 2 * BLK), 0)
    ki = lax.broadcasted_iota(jnp.int32, (BLK, 2 * BLK), 1)
    dist = BLK + qi - ki
    band = (dist >= 0) & (dist <= BLK)
    first = _head_masks(BLK)
    lane = lax.broadcasted_iota(jnp.int32, (BLK, LANES), 1)
    zero = jnp.zeros((BLK, LANES), BF16)

    def block(i, carry):
        r0 = pl.multiple_of(i * BLK, BLK)
        mask = band & ((ki >= BLK) | (n + i > 0))
        for p in range(N_HEADS // 2):
            sl = slice(p * LANES, (p + 1) * LANES)
            qp = q_ref[0, pl.ds(r0, BLK), sl]
            kp = kf_ref[pl.ds(r0, 2 * BLK), sl]
            s_ref[2 * p] = jnp.where(mask, _qk(jnp.where(first, qp, zero), kp), -jnp.inf)
            s_ref[2 * p + 1] = jnp.where(mask, _qk(jnp.where(first, zero, qp), kp), -jnp.inf)
        s = s_ref[...]
        m = jnp.max(s, axis=-1, keepdims=True)
        e = jnp.exp(s - m)
        l = jnp.sum(e, axis=-1, keepdims=True)
        p_ref[...] = e.astype(BF16)
        inv = 1.0 / l
        lse = m + jnp.log(l)
        lse_tile = jnp.zeros((BLK, LANES), F32)
        for p in range(N_HEADS // 2):
            sl = slice(p * LANES, (p + 1) * LANES)
            vp = vf_ref[pl.ds(r0, 2 * BLK), sl]
            oa = jnp.dot(p_ref[2 * p], vp, preferred_element_type=F32) * inv[2 * p]
            ob = jnp.dot(p_ref[2 * p + 1], vp, preferred_element_type=F32) * inv[2 * p + 1]
            o_ref[0, pl.ds(r0, BLK), sl] = jnp.where(first, oa, ob).astype(BF16)
            lse_tile = jnp.where(lane == 2 * p, lse[2 * p], lse_tile)
            lse_tile = jnp.where(lane == 2 * p + 1, lse[2 * p + 1], lse_tile)
        lse_ref[0, pl.ds(r0, BLK), :] = lse_tile
        return carry

    lax.fori_loop(0, tq // BLK, block, 0)


def _band_attention(q, kv, g, *, B, S):
    r = DILATIONS[g]
    L = S // r
    tq = min(L, 4 * BLK)
    qv = q.reshape(B, L, r * N_GROUPS * D_MODEL)
    kvv = kv.reshape(B, L, r * 2 * N_GROUPS * D_MODEL)
    o, lse = pl.pallas_call(
        functools.partial(_band_attn_kernel, tq=tq),
        grid=(B, r, L // tq),
        in_specs=[
            pl.BlockSpec((1, tq, D_MODEL), lambda b, c, n: (b, n, c * N_GROUPS + g)),
            pl.BlockSpec((1, tq, D_MODEL), lambda b, c, n: (b, n, c * 2 * N_GROUPS + 2 * g)),
            pl.BlockSpec((1, tq, D_MODEL), lambda b, c, n: (b, n, c * 2 * N_GROUPS + 2 * g + 1)),
        ],
        out_specs=[
            pl.BlockSpec((1, tq, D_MODEL), lambda b, c, n: (b, n, c)),
            pl.BlockSpec((1, tq, LANES), lambda b, c, n: (b, n, c)),
        ],
        out_shape=[jax.ShapeDtypeStruct((B, L, r * D_MODEL), BF16),
                   jax.ShapeDtypeStruct((B, L, r * LANES), F32)],
        scratch_shapes=[pltpu.VMEM((BLK + tq, D_MODEL), BF16), pltpu.VMEM((BLK + tq, D_MODEL), BF16),
                        pltpu.VMEM((N_HEADS, BLK, 2 * BLK), F32), pltpu.VMEM((N_HEADS, BLK, 2 * BLK), BF16)],
        compiler_params=_params("parallel", "parallel", "arbitrary"),
        name=f"band_attention_g{g}",
    )(qv, kvv, kvv)
    return o.reshape(B * S, D_MODEL), lse.reshape(B * S, LANES)


NEW_PAD = LANES


def _cache_attn_kernel(q_ref, kvn_ref, c_ref, o_ref, lse_ref, new_ref, *, lb, r, t, keep):
    lr = c_ref.shape[1]
    new_ref[...] = jnp.zeros((NEW_PAD, 2 * D_MODEL), BF16)
    new_ref[0:t, :] = kvn_ref[0]

    ti = lax.broadcasted_iota(jnp.int32, (t, NEW_PAD + lr), 0)
    rho = lax.broadcasted_iota(jnp.int32, (t, NEW_PAD + lr), 1)
    nrow = rho - NEW_PAD
    crow = nrow if keep == r else r * (nrow // keep) + nrow % keep
    idx = jnp.where(rho < NEW_PAD, lb + rho, crow)
    d = lb + ti - idx
    mask = ((rho < t) | (rho >= NEW_PAD)) & (d >= 0) & (d <= BLK * r) & ((d & (r - 1)) == 0)
    first = _head_masks(t)
    lane = lax.broadcasted_iota(jnp.int32, (t, LANES), 1)
    zero = jnp.zeros((t, LANES), BF16)

    lse_tile = jnp.zeros((t, LANES), F32)
    for p in range(N_HEADS // 2):
        ks = slice(p * LANES, (p + 1) * LANES)
        vs = slice(D_MODEL + p * LANES, D_MODEL + (p + 1) * LANES)
        qp = q_ref[0, :, ks]
        outs = []
        for hh, qm in enumerate((jnp.where(first, qp, zero), jnp.where(first, zero, qp))):
            s = jnp.concatenate([_qk(qm, new_ref[:, ks]), _qk(qm, c_ref[0, :, ks])], axis=1)
            s = jnp.where(mask, s, -jnp.inf)
            m = jnp.max(s, axis=-1, keepdims=True)
            e = jnp.exp(s - m)
            l = jnp.sum(e, axis=-1, keepdims=True)
            eb = e.astype(BF16)
            o = (jnp.dot(eb[:, :NEW_PAD], new_ref[:, vs], preferred_element_type=F32)
                 + jnp.dot(eb[:, NEW_PAD:], c_ref[0, :, vs], preferred_element_type=F32)) * (1.0 / l)
            outs.append(o)
            lse_tile = jnp.where(lane == 2 * p + hh, m + jnp.log(l), lse_tile)
        o_ref[0, :, ks] = jnp.where(first, outs[0], outs[1])
    lse_ref[0] = lse_tile


def _cache_attention(q, kvn, cache, g, *, B, T, keep):
    lr = cache.shape[1]
    r = DILATIONS[g]
    lb = lr * r // keep
    o, lse = pl.pallas_call(
        functools.partial(_cache_attn_kernel, lb=lb, r=r, t=T, keep=keep),
        grid=(B,),
        in_specs=[
            pl.BlockSpec((1, T, D_MODEL), lambda b: (b, 0, g)),
            pl.BlockSpec((1, T, 2 * D_MODEL), lambda b: (b, 0, g)),
            pl.BlockSpec((1, lr, 2 * D_MODEL), lambda b: (b, 0, 0)),
        ],
        out_specs=[
            pl.BlockSpec((1, T, D_MODEL), lambda b: (b, 0, 0)),
            pl.BlockSpec((1, T, LANES), lambda b: (b, 0, 0)),
        ],
        out_shape=[jax.ShapeDtypeStruct((B, T, D_MODEL), F32), jax.ShapeDtypeStruct((B, T, LANES), F32)],
        scratch_shapes=[pltpu.VMEM((NEW_PAD, 2 * D_MODEL), BF16)],
        compiler_params=_params("parallel"),
        name=f"cache_attention_g{g}",
    )(q, kvn, cache)
    return o.reshape(B * T, D_MODEL), lse.reshape(B * T, LANES)


def _cache_roll_kernel(c_hbm, new_hbm, o_hbm, sem, *, nb, lb, t):
    def old(b):
        return pltpu.make_async_copy(c_hbm.at[b, pl.ds(t, lb - t)], o_hbm.at[b, pl.ds(0, lb - t)], sem.at[0, b])

    def new(b):
        return pltpu.make_async_copy(new_hbm.at[b], o_hbm.at[b, pl.ds(lb - t, t)], sem.at[1, b])

    for b in range(nb):
        old(b).start()
        new(b).start()
    for b in range(nb):
        old(b).wait()
        new(b).wait()


def _cache_roll(cache, new):
    nb, lb = cache.shape[:2]
    t = new.shape[1]
    return pl.pallas_call(
        functools.partial(_cache_roll_kernel, nb=nb, lb=lb, t=t),
        in_specs=[pl.BlockSpec(memory_space=pl.ANY), pl.BlockSpec(memory_space=pl.ANY)],
        out_specs=pl.BlockSpec(memory_space=pl.ANY),
        out_shape=jax.ShapeDtypeStruct(cache.shape, cache.dtype),
        scratch_shapes=[pltpu.SemaphoreType.DMA((2, nb))],
        name="cache_roll",
    )(cache, new)


def _merge_kernel(*refs, n):
    x_ref = refs[0]
    o_refs = refs[1:1 + n]
    l_refs = refs[1 + n:1 + 2 * n]
    wo_ref, y_ref = refs[1 + 2 * n:]
    ls = [r[...] for r in l_refs]
    m = functools.reduce(jnp.maximum, ls)
    es = [jnp.exp(l - m) for l in ls]
    inv = 1.0 / functools.reduce(lambda a, b: a + b, es)
    ws = [e * inv for e in es]
    first = _head_masks(x_ref.shape[0])
    cols = []
    for p in range(N_HEADS // 2):
        sl = slice(p * LANES, (p + 1) * LANES)
        acc = None
        for w, o_ref in zip(ws, o_refs):
            wp = jnp.where(first, w[:, 2 * p:2 * p + 1], w[:, 2 * p + 1:2 * p + 2])
            term = wp * o_ref[:, sl].astype(F32)
            acc = term if acc is None else acc + term
        cols.append(acc.astype(BF16))
    merged = jnp.concatenate(cols, axis=-1)
    y_ref[...] = x_ref[...] + jnp.dot(merged, wo_ref[...], preferred_element_type=F32)


def _merge_layer(x, outs, lses, wo, *, tm):
    M = x.shape[0]
    n = len(outs)
    row = lambda w: pl.BlockSpec((tm, w), lambda i: (i, 0))
    return pl.pallas_call(
        functools.partial(_merge_kernel, n=n),
        grid=(M // tm,),
        in_specs=[row(D_MODEL)] * (1 + n) + [row(LANES)] * n + [pl.BlockSpec((D_MODEL, D_MODEL), lambda i: (0, 0))],
        out_specs=row(D_MODEL),
        out_shape=jax.ShapeDtypeStruct((M, D_MODEL), F32),
        compiler_params=_params("parallel"),
        name="merge_groups_wo",
    )(x, *outs, *lses, wo)


def _compact_cache(cache, g, T):
    B, lb = cache.shape[:2]
    r = DILATIONS[g]
    keep = min(r, T)
    c = cache.reshape(B, lb // r, r, 2 * D_MODEL)[:, :, :keep]
    return c.reshape(B, (lb // r) * keep, 2 * D_MODEL).astype(BF16), keep


def _trunk(x, pos, conv_prev, kv_caches, w, *, bb, ts, tm):
    B, S, _ = x.shape
    M = B * S
    cos, sin = _rope_tables(pos)
    if S < tm:
        cos, sin = jnp.tile(cos, (tm // S, 1)), jnp.tile(sin, (tm // S, 1))
    prev32 = jnp.pad(conv_prev, ((0, 0), (0, 0), (CONV_HALO - (CONV_WIDTH - 1), 0), (0, 0)))

    conv_states = []
    new_bufs = []
    kv = None
    compact = None
    for layer in range(DEPTH):
        if layer < N_A:
            x, st = _conv_layer(x, prev32[layer], w['a_norm'][layer], w['a_w_in'][layer], w['a_b_in'][layer],
                                w['a_w_dw'][layer], w['a_b_dw'][layer], w['a_ln_g'][layer], w['a_ln_b'][layer],
                                w['a_w_out'][layer], w['a_b_out'][layer], bb=bb, ts=ts)
            conv_states.append(st[:, CONV_HALO - (CONV_WIDTH - 1):])
            xf = x.reshape(M, D_MODEL)
        else:
            j = layer - N_A
            if layer == N_A:
                if kv_caches is None:
                    seq, tails = S, tuple(min(wd, S) for wd in WINDOWS)
                else:
                    seq, tails = M, (M,) * N_GROUPS
                kv, *tail_out = _proj(xf, w['kv_norm'], w['w_kv'], w['k_gain'], cos, sin, w['head_ones'],
                                      tm=tm, plain_every=2, seq=seq, tails=tails)
                for g in range(N_GROUPS):
                    if kv_caches is None:
                        new_bufs.append(tail_out[g].reshape(B, tails[g], 2, N_HEADS, HEAD_DIM))
                    else:
                        new_bufs.append(_cache_roll(kv_caches[g], tail_out[g].reshape(B, S, 2, N_HEADS, HEAD_DIM)))
                if kv_caches is not None:
                    compact = [_compact_cache(kv_caches[g], g, S) for g in range(N_GROUPS)]
            q = _proj(xf, w['b_norm'][j], w['w_q'][j], w['q_gain'][j], cos, sin, w['head_ones'],
                      tm=tm, plain_every=0)[0]
            outs, lses = [], []
            for g in range(N_GROUPS):
                if kv_caches is None:
                    o, l = _band_attention(q, kv, g, B=B, S=S)
                else:
                    o, l = _cache_attention(q.reshape(B, S, -1), kv.reshape(B, S, -1), compact[g][0], g,
                                            B=B, T=S, keep=compact[g][1])
                outs.append(o)
                lses.append(l)
            xf = _merge_layer(xf, outs, lses, w['w_o'][j], tm=tm)
        xf = _ffn_layer(xf, w['ffn_norm'][layer], w['w_gate_up'][layer], w['w_down'][layer], tm=tm)
        x = xf.reshape(B, S, D_MODEL)
    return x, jnp.stack(conv_states, axis=0), new_bufs


def kernel(x_prompt, x_sample, cache_conv, cache_kv_w128, cache_kv_w512, cache_kv_w2048,
           a_norm, a_w_in, a_b_in, a_w_dw, a_b_dw, a_ln_g, a_ln_b, a_w_out, a_b_out,
           kv_norm, w_kv, k_norm, b_norm, w_q, q_norm, w_o, ffn_norm, w_gate_up, w_down):
    B, S, _ = x_prompt.shape
    Bd, T, _ = x_sample.shape
    row = lambda a: a.reshape(a.shape[:-1] + (1, a.shape[-1]))
    head_of = jnp.arange(MXU_N, dtype=jnp.int32) // HEAD_DIM
    w = dict(
        a_norm=row(a_norm), a_w_in=a_w_in.astype(BF16), a_b_in=row(a_b_in), a_w_dw=a_w_dw, a_b_dw=row(a_b_dw),
        a_ln_g=row(a_ln_g), a_ln_b=row(a_ln_b), a_w_out=a_w_out.astype(BF16), a_b_out=row(a_b_out),
        kv_norm=row(kv_norm), w_kv=w_kv.astype(BF16), b_norm=row(b_norm), w_q=w_q.astype(BF16),
        w_o=w_o.astype(BF16), ffn_norm=row(ffn_norm), w_gate_up=w_gate_up.astype(BF16), w_down=w_down.astype(BF16),
        k_gain=row(jnp.tile(k_norm, (1, N_HEADS))),
        q_gain=row(jnp.tile(q_norm, (1, 1, N_HEADS)) * SCALE),
        head_ones=(head_of[:, None] == head_of[None, :]).astype(BF16),
    )
    pos_prompt = jnp.arange(S, dtype=jnp.int32)
    pos_sample = PAST_LEN + jnp.arange(T, dtype=jnp.int32)
    conv_zero = jnp.zeros((N_A, B, CONV_WIDTH - 1, D_MODEL), x_prompt.dtype)
    y_p, conv_p, bufs_p = _trunk(x_prompt, pos_prompt, conv_zero, None, w, bb=1, ts=512, tm=512)
    y_s, conv_s, bufs_s = _trunk(x_sample, pos_sample, cache_conv,
                                 (cache_kv_w128, cache_kv_w512, cache_kv_w2048), w, bb=Bd, ts=T, tm=Bd * T)
    return (y_p, y_s, conv_p, conv_s, bufs_p[0], bufs_p[1], bufs_p[2], bufs_s[0], bufs_s[1], bufs_s[2])
```

```python
import functools

import jax
import jax.numpy as jnp
from jax import lax
from jax.experimental import pallas as pl
from jax.experimental.pallas import tpu as pltpu

D_MODEL = 1024
DEPTH = 4
PAST_LEN = 8192
N_A = DEPTH // 2
N_B = DEPTH - N_A
CONV_WIDTH = 31
CONV_HALO = 32
SUBLANES = 8
LANES = 128
N_HEADS = 16
HEAD_DIM = D_MODEL // N_HEADS
WINDOWS = (128, 512, 2048)
DILATIONS = (1, 4, 16)
N_GROUPS = len(WINDOWS)
BLK = 128
D_FF = (8 * D_MODEL + 3 * 256 - 1) // (3 * 256) * 256
ROPE_THETA = 10000.0
EPS = 1e-6
SCALE = HEAD_DIM ** -0.5
MXU_N = 256
PROJ_SLAB = 2 * MXU_N
VMEM_LIMIT = 56 * 1024 * 1024

F32 = jnp.float32
BF16 = jnp.bfloat16


def _params(*sem):
    return pltpu.CompilerParams(dimension_semantics=sem, vmem_limit_bytes=VMEM_LIMIT)


def _rms(x, g):
    return x * lax.rsqrt(jnp.mean(x * x, axis=-1, keepdims=True) + EPS) * g


def _sigmoid(x):
    return 1.0 / (1.0 + jnp.exp(-x))


def _conv_kernel(x_ref, prev_ref, ng_ref, win_ref, bin_ref, wdw_ref, bdw_ref, lng_ref, lnb_ref,
                 wout_ref, bout_ref, y_ref, st_ref, ext_ref, *, ts):
    s = pl.program_id(1)
    bb = x_ref.shape[0]

    @pl.when(s == 0)
    def _():
        ext_ref[:, 0:CONV_HALO, :] = prev_ref[...]

    @pl.when(s > 0)
    def _():
        ext_ref[:, 0:CONV_HALO, :] = ext_ref[:, ts:ts + CONV_HALO, :]

    x = x_ref[...].reshape(bb * ts, D_MODEL)
    u = _rms(x, ng_ref[...]).astype(BF16)
    a = jnp.dot(u, win_ref[...], preferred_element_type=F32) + bin_ref[...]
    glu = a[:, :D_MODEL] * _sigmoid(a[:, D_MODEL:])
    ext_ref[:, CONV_HALO:CONV_HALO + ts, :] = glu.reshape(bb, ts, D_MODEL)

    off = CONV_HALO - (CONV_WIDTH - 1)
    c = None
    for b in range(SUBLANES):
        rows = ts if b == 0 else ts + SUBLANES
        yb = None
        for a_ in range((off + CONV_WIDTH - 1) // SUBLANES + 1):
            j = SUBLANES * a_ + b - off
            if 0 <= j < CONV_WIDTH:
                term = ext_ref[:, SUBLANES * a_:SUBLANES * a_ + rows, :] * wdw_ref[j:j + 1, :]
                yb = term if yb is None else yb + term
        yb = yb[:, b:b + ts, :]
        c = yb if c is None else c + yb
    c = c.reshape(bb * ts, D_MODEL) + bdw_ref[...]
    mu = jnp.mean(c, axis=-1, keepdims=True)
    cc = c - mu
    yn = cc * lax.rsqrt(jnp.mean(cc * cc, axis=-1, keepdims=True) + EPS) * lng_ref[...] + lnb_ref[...]
    h = (yn * _sigmoid(yn)).astype(BF16)
    out = x + jnp.dot(h, wout_ref[...], preferred_element_type=F32) + bout_ref[...]
    y_ref[...] = out.reshape(bb, ts, D_MODEL)
    st_ref[...] = ext_ref[:, ts:ts + CONV_HALO, :]


def _conv_layer(x, prev32, ng, win, bin_, wdw, bdw, lng, lnb, wout, bout, *, bb, ts):
    B, S, _ = x.shape
    full = lambda shape: pl.BlockSpec(shape, lambda b, s: (0,) * len(shape))
    return pl.pallas_call(
        functools.partial(_conv_kernel, ts=ts),
        grid=(B // bb, S // ts),
        in_specs=[
            pl.BlockSpec((bb, ts, D_MODEL), lambda b, s: (b, s, 0)),
            pl.BlockSpec((bb, CONV_HALO, D_MODEL), lambda b, s: (b, 0, 0)),
            full((1, D_MODEL)), full((D_MODEL, 2 * D_MODEL)), full((1, 2 * D_MODEL)),
            full((CONV_WIDTH, D_MODEL)), full((1, D_MODEL)), full((1, D_MODEL)), full((1, D_MODEL)),
            full((D_MODEL, D_MODEL)), full((1, D_MODEL)),
        ],
        out_specs=[
            pl.BlockSpec((bb, ts, D_MODEL), lambda b, s: (b, s, 0)),
            pl.BlockSpec((bb, CONV_HALO, D_MODEL), lambda b, s: (b, 0, 0)),
        ],
        out_shape=[jax.ShapeDtypeStruct((B, S, D_MODEL), F32),
                   jax.ShapeDtypeStruct((B, CONV_HALO, D_MODEL), F32)],
        scratch_shapes=[pltpu.VMEM((bb, CONV_HALO + ts, D_MODEL), F32)],
        compiler_params=_params("parallel", "arbitrary"),
        name="conv_module",
    )(x, prev32, ng, win, bin_, wdw, bdw, lng, lnb, wout, bout)


FF_SLAB = 256


def _ffn_kernel(x_ref, g_ref, wgu_ref, wd_ref, y_ref, h_ref):
    x = x_ref[...]
    xn = _rms(x, g_ref[...]).astype(BF16)
    for k in range(D_FF // FF_SLAB):
        gate = jnp.dot(xn, wgu_ref[:, k * FF_SLAB:(k + 1) * FF_SLAB], preferred_element_type=F32)
        up = jnp.dot(xn, wgu_ref[:, D_FF + k * FF_SLAB:D_FF + (k + 1) * FF_SLAB], preferred_element_type=F32)
        h_ref[:, k * FF_SLAB:(k + 1) * FF_SLAB] = (gate * _sigmoid(gate) * up).astype(BF16)
    y_ref[...] = x + jnp.dot(h_ref[...], wd_ref[...], preferred_element_type=F32)


def _ffn_layer(x, g, wgu, wd, *, tm):
    M = x.shape[0]
    resident = lambda shape: pl.BlockSpec(shape, lambda i: (0, 0), pipeline_mode=pl.Buffered(1))
    return pl.pallas_call(
        _ffn_kernel,
        grid=(M // tm,),
        in_specs=[
            pl.BlockSpec((tm, D_MODEL), lambda i: (i, 0)),
            pl.BlockSpec((1, D_MODEL), lambda i: (0, 0)),
            resident((D_MODEL, 2 * D_FF)),
            resident((D_FF, D_MODEL)),
        ],
        out_specs=pl.BlockSpec((tm, D_MODEL), lambda i: (i, 0)),
        out_shape=jax.ShapeDtypeStruct((M, D_MODEL), F32),
        scratch_shapes=[pltpu.VMEM((tm, D_FF), BF16)],
        compiler_params=_params("parallel"),
        name="swiglu_ffn",
    )(x, g, wgu, wd)


def _normed_slab(y, gain, cos, sin, e):
    ssq = jnp.dot((y * y).astype(BF16), e, preferred_element_type=F32)
    yn = y * lax.rsqrt(ssq * (1.0 / HEAD_DIM) + EPS) * gain
    lane = lax.broadcasted_iota(jnp.int32, (y.shape[0], LANES), 1)
    lo_half = (lane % HEAD_DIM) < (HEAD_DIM // 2)
    outs = []
    for k in range(MXU_N // LANES):
        z = yn[:, k * LANES:(k + 1) * LANES]
        partner = jnp.where(lo_half, pltpu.roll(z, LANES - HEAD_DIM // 2, 1), pltpu.roll(z, HEAD_DIM // 2, 1))
        outs.append(z * cos + partner * sin)
    return jnp.concatenate(outs, axis=-1)


def _proj_kernel(*refs, tm, n_chunks, plain_every, tails, n_s):
    x_ref, g_ref, w_ref, gain_ref, cos_ref, sin_ref, e_ref = refs[:7]
    ob_refs = refs[7:7 + n_chunks]
    tail_refs = refs[7 + n_chunks:7 + n_chunks + len(tails)]
    xn_ref = refs[-1]
    i = pl.program_id(0)
    c = pl.program_id(1)
    s = i % n_s

    @pl.when(c == 0)
    def _():
        xn_ref[...] = _rms(x_ref[...], g_ref[...]).astype(BF16)

    def chunk(ob_ref, normed, tail_ref, tail_len, col0):
        xn = xn_ref[...]
        for k in range(D_MODEL // PROJ_SLAB):
            y = jnp.dot(xn, w_ref[:, k * PROJ_SLAB:(k + 1) * PROJ_SLAB], preferred_element_type=F32)
            for h in range(PROJ_SLAB // MXU_N):
                sl = slice(k * PROJ_SLAB + h * MXU_N, k * PROJ_SLAB + (h + 1) * MXU_N)
                z = y[:, h * MXU_N:(h + 1) * MXU_N]
                if normed:
                    z = _normed_slab(z, gain_ref[0, :, sl], cos_ref[...], sin_ref[...], e_ref[...])
                ob_ref[:, sl] = z.astype(BF16)
                if tail_ref is not None:
                    tsl = slice(col0 + sl.start, col0 + sl.stop)
                    if tail_len >= tm:
                        @pl.when(s >= n_s - tail_len // tm)
                        def _():
                            tail_ref[0, :, tsl] = z
                    else:
                        @pl.when(s == n_s - 1)
                        def _():
                            tail_ref[0, :, tsl] = z[tm - tail_len:, :]

    for cs in range(n_chunks):
        g, part = divmod(cs, max(plain_every, 1))
        tail = (tail_refs[g], tails[g]) if tails else (None, 0)
        pl.when(c == cs)(functools.partial(chunk, ob_refs[cs], part == 0, *tail, part * D_MODEL))


def _proj(x, g, w, gain, cos, sin, e, *, tm, plain_every, seq=None, tails=()):
    M = x.shape[0]
    n_chunks = w.shape[1] // D_MODEL
    n_pos_tiles = cos.shape[0] // tm
    gdiv = max(plain_every, 1)
    seq = seq or M
    n_s = seq // tm
    out_shape = [jax.ShapeDtypeStruct((M, D_MODEL), BF16)] * n_chunks
    out_specs = [pl.BlockSpec((tm, D_MODEL), lambda i, c: (i, 0))] * n_chunks
    for lg in tails:
        out_shape.append(jax.ShapeDtypeStruct((M // seq, lg, 2 * D_MODEL), F32))
        if lg >= tm:
            first = n_s - lg // tm
            out_specs.append(pl.BlockSpec((1, tm, 2 * D_MODEL),
                                          lambda i, c, first=first: (i // n_s, jnp.maximum(i % n_s - first, 0), 0)))
        else:
            out_specs.append(pl.BlockSpec((1, lg, 2 * D_MODEL), lambda i, c: (i // n_s, 0, 0)))
    return pl.pallas_call(
        functools.partial(_proj_kernel, tm=tm, n_chunks=n_chunks, plain_every=plain_every, tails=tuple(tails), n_s=n_s),
        grid=(M // tm, n_chunks),
        in_specs=[
            pl.BlockSpec((tm, D_MODEL), lambda i, c: (i, 0)),
            pl.BlockSpec((1, D_MODEL), lambda i, c: (0, 0)),
            pl.BlockSpec((D_MODEL, D_MODEL), lambda i, c: (0, c)),
            pl.BlockSpec((1, 1, D_MODEL), lambda i, c: (c // gdiv, 0, 0)),
            pl.BlockSpec((tm, LANES), lambda i, c: (i % n_pos_tiles, 0)),
            pl.BlockSpec((tm, LANES), lambda i, c: (i % n_pos_tiles, 0)),
            pl.BlockSpec((MXU_N, MXU_N), lambda i, c: (0, 0)),
        ],
        out_specs=out_specs,
        out_shape=out_shape,
        scratch_shapes=[pltpu.VMEM((tm, D_MODEL), BF16)],
        compiler_params=_params("arbitrary", "arbitrary"),
        name="norm_proj_rotary",
    )(x, g, w, gain, cos, sin, e)


def _rope_tables(pos):
    half = HEAD_DIM // 2
    inv = ROPE_THETA ** (-jnp.arange(half, dtype=F32) / half)
    ang = pos.astype(F32)[:, None] * inv[None, :]
    cos, sin = jnp.cos(ang), jnp.sin(ang)
    reps = LANES // HEAD_DIM
    return jnp.tile(cos, (1, 2 * reps)), jnp.tile(jnp.concatenate([-sin, sin], axis=1), (1, reps))


def _qk(q, k):
    return lax.dot_general(q, k, (((1,), (1,)), ((), ())), preferred_element_type=F32)


def _head_masks(rows):
    first = lax.broadcasted_iota(jnp.int32, (rows, LANES), 1) < HEAD_DIM
    return first


def _band_attn_kernel(q_ref, k_ref, v_ref, o_ref, lse_ref, kf_ref, vf_ref, s_ref, p_ref, *, tq):
    n = pl.program_id(2)

    @pl.when(n == 0)
    def _():
        kf_ref[0:BLK, :] = jnp.zeros((BLK, D_MODEL), BF16)
        vf_ref[0:BLK, :] = jnp.zeros((BLK, D_MODEL), BF16)

    @pl.when(n > 0)
    def _():
        kf_ref[0:BLK, :] = kf_ref[tq:tq + BLK, :]
        vf_ref[0:BLK, :] = vf_ref[tq:tq + BLK, :]

    kf_ref[BLK:BLK + tq, :] = k_ref[0]
    vf_ref[BLK:BLK + tq, :] = v_ref[0]

    qi = lax.broadcasted_iota(jnp.int32, (BLK, 2 * BLK), 0)
    ki = lax.broadcasted_iota(jnp.int32, (BLK, 2 * BLK), 1)
    dist = BLK + qi - ki
    band = (dist >= 0) & (dist <= BLK)
    first = _head_masks(BLK)
    lane = lax.broadcasted_iota(jnp.int32, (BLK, LANES), 1)
    zero = jnp.zeros((BLK, LANES), BF16)

    def block(i, carry):
        r0 = pl.multiple_of(i * BLK, BLK)
        mask = band & ((ki >= BLK) | (n + i > 0))
        for p in range(N_HEADS // 2):
            sl = slice(p * LANES, (p + 1) * LANES)
            qp = q_ref[0, pl.ds(r0, BLK), sl]
            kp = kf_ref[pl.ds(r0, 2 * BLK), sl]
            s_ref[2 * p] = jnp.where(mask, _qk(jnp.where(first, qp, zero), kp), -jnp.inf)
            s_ref[2 * p + 1] = jnp.where(mask, _qk(jnp.where(first, zero, qp), kp), -jnp.inf)
        s = s_ref[...]
        m = jnp.max(s, axis=-1, keepdims=True)
        e = jnp.exp(s - m)
        l = jnp.sum(e, axis=-1, keepdims=True)
        p_ref[...] = e.astype(BF16)
        inv = 1.0 / l
        lse = m + jnp.log(l)
        lse_tile = jnp.zeros((BLK, LANES), F32)
        for p in range(N_HEADS // 2):
            sl = slice(p * LANES, (p + 1) * LANES)
            vp = vf_ref[pl.ds(r0, 2 * BLK), sl]
            oa = jnp.dot(p_ref[2 * p], vp, preferred_element_type=F32) * inv[2 * p]
            ob = jnp.dot(p_ref[2 * p + 1], vp, preferred_element_type=F32) * inv[2 * p + 1]
            o_ref[0, pl.ds(r0, BLK), sl] = jnp.where(first, oa, ob).astype(BF16)
            lse_tile = jnp.where(lane == 2 * p, lse[2 * p], lse_tile)
            lse_tile = jnp.where(lane == 2 * p + 1, lse[2 * p + 1], lse_tile)
        lse_ref[0, pl.ds(r0, BLK), :] = lse_tile
        return carry

    lax.fori_loop(0, tq // BLK, block, 0)


def _band_attention(q, k, v, g, *, B, S):
    r = DILATIONS[g]
    L = S // r
    tq = min(L, 4 * BLK)
    view = lambda a: a.reshape(B, L, r * D_MODEL)
    spec = pl.BlockSpec((1, tq, D_MODEL), lambda b, c, n: (b, n, c))
    o, lse = pl.pallas_call(
        functools.partial(_band_attn_kernel, tq=tq),
        grid=(B, r, L // tq),
        in_specs=[spec, spec, spec],
        out_specs=[
            pl.BlockSpec((1, tq, D_MODEL), lambda b, c, n: (b, n, c)),
            pl.BlockSpec((1, tq, LANES), lambda b, c, n: (b, n, c)),
        ],
        out_shape=[jax.ShapeDtypeStruct((B, L, r * D_MODEL), BF16),
                   jax.ShapeDtypeStruct((B, L, r * LANES), F32)],
        scratch_shapes=[pltpu.VMEM((BLK + tq, D_MODEL), BF16), pltpu.VMEM((BLK + tq, D_MODEL), BF16),
                        pltpu.VMEM((N_HEADS, BLK, 2 * BLK), F32), pltpu.VMEM((N_HEADS, BLK, 2 * BLK), BF16)],
        compiler_params=_params("parallel", "parallel", "arbitrary"),
        name=f"band_attention_g{g}",
    )(view(q), view(k), view(v))
    return o.reshape(B * S, D_MODEL), lse.reshape(B * S, LANES)


NEW_PAD = LANES


def _cache_attn_kernel(q_ref, kn_ref, vn_ref, c_ref, o_ref, lse_ref, new_ref, *, lb, r, t, keep):
    lr = c_ref.shape[1]
    new_ref[...] = jnp.zeros((NEW_PAD, 2 * D_MODEL), BF16)
    new_ref[0:t, 0:D_MODEL] = kn_ref[0]
    new_ref[0:t, D_MODEL:2 * D_MODEL] = vn_ref[0]

    ti = lax.broadcasted_iota(jnp.int32, (t, NEW_PAD + lr), 0)
    rho = lax.broadcasted_iota(jnp.int32, (t, NEW_PAD + lr), 1)
    nrow = rho - NEW_PAD
    crow = nrow if keep == r else r * (nrow // keep) + nrow % keep
    idx = jnp.where(rho < NEW_PAD, lb + rho, crow)
    d = lb + ti - idx
    mask = ((rho < t) | (rho >= NEW_PAD)) & (d >= 0) & (d <= BLK * r) & ((d & (r - 1)) == 0)
    first = _head_masks(t)
    lane = lax.broadcasted_iota(jnp.int32, (t, LANES), 1)
    zero = jnp.zeros((t, LANES), BF16)

    lse_tile = jnp.zeros((t, LANES), F32)
    for p in range(N_HEADS // 2):
        ks = slice(p * LANES, (p + 1) * LANES)
        vs = slice(D_MODEL + p * LANES, D_MODEL + (p + 1) * LANES)
        qp = q_ref[0, :, ks]
        outs = []
        for hh, qm in enumerate((jnp.where(first, qp, zero), jnp.where(first, zero, qp))):
            s = jnp.concatenate([_qk(qm, new_ref[:, ks]), _qk(qm, c_ref[0, :, ks])], axis=1)
            s = jnp.where(mask, s, -jnp.inf)
            m = jnp.max(s, axis=-1, keepdims=True)
            e = jnp.exp(s - m)
            l = jnp.sum(e, axis=-1, keepdims=True)
            eb = e.astype(BF16)
            o = (jnp.dot(eb[:, :NEW_PAD], new_ref[:, vs], preferred_element_type=F32)
                 + jnp.dot(eb[:, NEW_PAD:], c_ref[0, :, vs], preferred_element_type=F32)) * (1.0 / l)
            outs.append(o)
            lse_tile = jnp.where(lane == 2 * p + hh, m + jnp.log(l), lse_tile)
        o_ref[0, :, ks] = jnp.where(first, outs[0], outs[1])
    lse_ref[0] = lse_tile


def _cache_attention(q, kn, vn, cache, g, *, keep):
    B, T, _ = q.shape
    lr = cache.shape[1]
    r = DILATIONS[g]
    lb = lr * r // keep
    new = pl.BlockSpec((1, T, D_MODEL), lambda b: (b, 0, 0))
    o, lse = pl.pallas_call(
        functools.partial(_cache_attn_kernel, lb=lb, r=r, t=T, keep=keep),
        grid=(B,),
        in_specs=[new, new, new, pl.BlockSpec((1, lr, 2 * D_MODEL), lambda b: (b, 0, 0))],
        out_specs=[new, pl.BlockSpec((1, T, LANES), lambda b: (b, 0, 0))],
        out_shape=[jax.ShapeDtypeStruct((B, T, D_MODEL), F32), jax.ShapeDtypeStruct((B, T, LANES), F32)],
        scratch_shapes=[pltpu.VMEM((NEW_PAD, 2 * D_MODEL), BF16)],
        compiler_params=_params("parallel"),
        name=f"cache_attention_g{g}",
    )(q, kn, vn, cache)
    return o.reshape(B * T, D_MODEL), lse.reshape(B * T, LANES)


ROLL_COLS = 4 * LANES


def _cache_roll_kernel(c_ref, new_ref, o_ref, *, lb, t):
    o_ref[0, 0:lb - t, :] = c_ref[0, t:lb, :]
    o_ref[0, lb - t:lb, :] = new_ref[0]


def _cache_roll(cache, new):
    B, lb, C = cache.shape
    t = new.shape[1]
    return pl.pallas_call(
        functools.partial(_cache_roll_kernel, lb=lb, t=t),
        grid=(B, C // ROLL_COLS),
        in_specs=[
            pl.BlockSpec((1, lb, ROLL_COLS), lambda b, j: (b, 0, j)),
            pl.BlockSpec((1, t, ROLL_COLS), lambda b, j: (b, 0, j)),
        ],
        out_specs=pl.BlockSpec((1, lb, ROLL_COLS), lambda b, j: (b, 0, j)),
        out_shape=jax.ShapeDtypeStruct(cache.shape, cache.dtype),
        compiler_params=_params("parallel", "parallel"),
        name="cache_roll",
    )(cache, new)


def _merge_kernel(*refs, n):
    x_ref = refs[0]
    o_refs = refs[1:1 + n]
    l_refs = refs[1 + n:1 + 2 * n]
    wo_ref, y_ref = refs[1 + 2 * n:]
    ls = [r[...] for r in l_refs]
    m = functools.reduce(jnp.maximum, ls)
    es = [jnp.exp(l - m) for l in ls]
    inv = 1.0 / functools.reduce(lambda a, b: a + b, es)
    ws = [e * inv for e in es]
    first = _head_masks(x_ref.shape[0])
    cols = []
    for p in range(N_HEADS // 2):
        sl = slice(p * LANES, (p + 1) * LANES)
        acc = None
        for w, o_ref in zip(ws, o_refs):
            wp = jnp.where(first, w[:, 2 * p:2 * p + 1], w[:, 2 * p + 1:2 * p + 2])
            term = wp * o_ref[:, sl].astype(F32)
            acc = term if acc is None else acc + term
        cols.append(acc.astype(BF16))
    merged = jnp.concatenate(cols, axis=-1)
    y_ref[...] = x_ref[...] + jnp.dot(merged, wo_ref[...], preferred_element_type=F32)


def _merge_layer(x, outs, lses, wo, *, tm):
    M = x.shape[0]
    n = len(outs)
    row = lambda w: pl.BlockSpec((tm, w), lambda i: (i, 0))
    return pl.pallas_call(
        functools.partial(_merge_kernel, n=n),
        grid=(M // tm,),
        in_specs=[row(D_MODEL)] * (1 + n) + [row(LANES)] * n + [pl.BlockSpec((D_MODEL, D_MODEL), lambda i: (0, 0))],
        out_specs=row(D_MODEL),
        out_shape=jax.ShapeDtypeStruct((M, D_MODEL), F32),
        compiler_params=_params("parallel"),
        name="merge_groups_wo",
    )(x, *outs, *lses, wo)


def _compact_cache(cache, g, T):
    B, lb, C = cache.shape
    r = DILATIONS[g]
    keep = min(r, T)
    c = cache.reshape(B, lb // r, r, C)[:, :, :keep]
    return c.reshape(B, (lb // r) * keep, C).astype(BF16), keep


def _trunk(x, pos, conv_prev, kv_caches, w, *, bb, ts, tm):
    B, S, _ = x.shape
    M = B * S
    cos, sin = _rope_tables(pos)
    if S < tm:
        cos, sin = jnp.tile(cos, (tm // S, 1)), jnp.tile(sin, (tm // S, 1))
    prev32 = jnp.pad(conv_prev, ((0, 0), (0, 0), (CONV_HALO - (CONV_WIDTH - 1), 0), (0, 0)))

    conv_states = []
    new_bufs = []
    kv = None
    compact = None
    for layer in range(DEPTH):
        if layer < N_A:
            x, st = _conv_layer(x, prev32[layer], w['a_norm'][layer], w['a_w_in'][layer], w['a_b_in'][layer],
                                w['a_w_dw'][layer], w['a_b_dw'][layer], w['a_ln_g'][layer], w['a_ln_b'][layer],
                                w['a_w_out'][layer], w['a_b_out'][layer], bb=bb, ts=ts)
            conv_states.append(st[:, CONV_HALO - (CONV_WIDTH - 1):])
            xf = x.reshape(M, D_MODEL)
        else:
            j = layer - N_A
            if layer == N_A:
                if kv_caches is None:
                    seq, tails = S, tuple(min(wd, S) for wd in WINDOWS)
                else:
                    seq, tails = M, (M,) * N_GROUPS
                *kv, t0, t1, t2 = _proj(xf, w['kv_norm'], w['w_kv'], w['k_gain'], cos, sin, w['head_ones'],
                                        tm=tm, plain_every=2, seq=seq, tails=tails)
                for g, tail in enumerate((t0, t1, t2)):
                    if kv_caches is None:
                        buf = tail
                    else:
                        cache = kv_caches[g].reshape(B, -1, 2 * D_MODEL)
                        buf = _cache_roll(cache, tail.reshape(B, S, 2 * D_MODEL))
                        compact = (compact or []) + [_compact_cache(cache, g, S)]
                    new_bufs.append(buf.reshape(B, -1, 2, N_HEADS, HEAD_DIM))
            q = _proj(xf, w['b_norm'][j], w['w_q'][j], w['q_gain'][j], cos, sin, w['head_ones'],
                      tm=tm, plain_every=0)
            outs, lses = [], []
            for g in range(N_GROUPS):
                if kv_caches is None:
                    o, l = _band_attention(q[g], kv[2 * g], kv[2 * g + 1], g, B=B, S=S)
                else:
                    new = lambda a: a.reshape(B, S, D_MODEL)
                    o, l = _cache_attention(new(q[g]), new(kv[2 * g]), new(kv[2 * g + 1]), compact[g][0], g,
                                            keep=compact[g][1])
                outs.append(o)
                lses.append(l)
            xf = _merge_layer(xf, outs, lses, w['w_o'][j], tm=tm)
        xf = _ffn_layer(xf, w['ffn_norm'][layer], w['w_gate_up'][layer], w['w_down'][layer], tm=tm)
        x = xf.reshape(B, S, D_MODEL)
    return x, jnp.stack(conv_states, axis=0), new_bufs


def kernel(x_prompt, x_sample, cache_conv, cache_kv_w128, cache_kv_w512, cache_kv_w2048,
           a_norm, a_w_in, a_b_in, a_w_dw, a_b_dw, a_ln_g, a_ln_b, a_w_out, a_b_out,
           kv_norm, w_kv, k_norm, b_norm, w_q, q_norm, w_o, ffn_norm, w_gate_up, w_down):
    B, S, _ = x_prompt.shape
    Bd, T, _ = x_sample.shape
    row = lambda a: a.reshape(a.shape[:-1] + (1, a.shape[-1]))
    head_of = jnp.arange(MXU_N, dtype=jnp.int32) // HEAD_DIM
    w = dict(
        a_norm=row(a_norm), a_w_in=a_w_in.astype(BF16), a_b_in=row(a_b_in), a_w_dw=a_w_dw, a_b_dw=row(a_b_dw),
        a_ln_g=row(a_ln_g), a_ln_b=row(a_ln_b), a_w_out=a_w_out.astype(BF16), a_b_out=row(a_b_out),
        kv_norm=row(kv_norm), w_kv=w_kv.astype(BF16), b_norm=row(b_norm), w_q=w_q.astype(BF16),
        w_o=w_o.astype(BF16), ffn_norm=row(ffn_norm), w_gate_up=w_gate_up.astype(BF16), w_down=w_down.astype(BF16),
        k_gain=row(jnp.tile(k_norm, (1, N_HEADS))),
        q_gain=row(jnp.tile(q_norm, (1, 1, N_HEADS)) * SCALE),
        head_ones=(head_of[:, None] == head_of[None, :]).astype(BF16),
    )
    pos_prompt = jnp.arange(S, dtype=jnp.int32)
    pos_sample = PAST_LEN + jnp.arange(T, dtype=jnp.int32)
    conv_zero = jnp.zeros((N_A, B, CONV_WIDTH - 1, D_MODEL), x_prompt.dtype)
    y_p, conv_p, bufs_p = _trunk(x_prompt, pos_prompt, conv_zero, None, w, bb=1, ts=512, tm=512)
    y_s, conv_s, bufs_s = _trunk(x_sample, pos_sample, cache_conv,
                                 (cache_kv_w128, cache_kv_w512, cache_kv_w2048), w, bb=Bd, ts=T, tm=Bd * T)
    return (y_p, y_s, conv_p, conv_s, bufs_p[0], bufs_p[1], bufs_p[2], bufs_s[0], bufs_s[1], bufs_s[2])
```

```python
import functools

import jax
import jax.numpy as jnp
from jax import lax
from jax.experimental import pallas as pl
from jax.experimental.pallas import tpu as pltpu

D_MODEL = 1024
DEPTH = 4
PAST_LEN = 8192
N_A = DEPTH // 2
N_B = DEPTH - N_A
CONV_WIDTH = 31
CONV_HALO = 32
SUBLANES = 8
LANES = 128
N_HEADS = 16
HEAD_DIM = D_MODEL // N_HEADS
WINDOWS = (128, 512, 2048)
DILATIONS = (1, 4, 16)
N_GROUPS = len(WINDOWS)
BLK = 128
D_FF = (8 * D_MODEL + 3 * 256 - 1) // (3 * 256) * 256
ROPE_THETA = 10000.0
EPS = 1e-6
SCALE = HEAD_DIM ** -0.5
MXU_N = 256
PROJ_SLAB = 2 * MXU_N
VMEM_LIMIT = 56 * 1024 * 1024

F32 = jnp.float32
BF16 = jnp.bfloat16


def _params(*sem):
    return pltpu.CompilerParams(dimension_semantics=sem, vmem_limit_bytes=VMEM_LIMIT)


def _rms(x, g):
    return x * lax.rsqrt(jnp.mean(x * x, axis=-1, keepdims=True) + EPS) * g


def _sigmoid(x):
    return 1.0 / (1.0 + jnp.exp(-x))


def _conv_kernel(x_ref, prev_ref, ng_ref, win_ref, bin_ref, wdw_ref, bdw_ref, lng_ref, lnb_ref,
                 wout_ref, bout_ref, y_ref, st_ref, ext_ref, *, ts):
    s = pl.program_id(1)
    bb = x_ref.shape[0]

    @pl.when(s == 0)
    def _():
        ext_ref[:, 0:CONV_HALO, :] = prev_ref[...]

    @pl.when(s > 0)
    def _():
        ext_ref[:, 0:CONV_HALO, :] = ext_ref[:, ts:ts + CONV_HALO, :]

    x = x_ref[...].reshape(bb * ts, D_MODEL)
    u = _rms(x, ng_ref[...]).astype(BF16)
    a = jnp.dot(u, win_ref[...], preferred_element_type=F32) + bin_ref[...]
    glu = a[:, :D_MODEL] * _sigmoid(a[:, D_MODEL:])
    ext_ref[:, CONV_HALO:CONV_HALO + ts, :] = glu.reshape(bb, ts, D_MODEL)

    off = CONV_HALO - (CONV_WIDTH - 1)
    c = None
    for b in range(SUBLANES):
        rows = ts if b == 0 else ts + SUBLANES
        yb = None
        for a_ in range((off + CONV_WIDTH - 1) // SUBLANES + 1):
            j = SUBLANES * a_ + b - off
            if 0 <= j < CONV_WIDTH:
                term = ext_ref[:, SUBLANES * a_:SUBLANES * a_ + rows, :] * wdw_ref[j:j + 1, :]
                yb = term if yb is None else yb + term
        yb = yb[:, b:b + ts, :]
        c = yb if c is None else c + yb
    c = c.reshape(bb * ts, D_MODEL) + bdw_ref[...]
    mu = jnp.mean(c, axis=-1, keepdims=True)
    cc = c - mu
    yn = cc * lax.rsqrt(jnp.mean(cc * cc, axis=-1, keepdims=True) + EPS) * lng_ref[...] + lnb_ref[...]
    h = (yn * _sigmoid(yn)).astype(BF16)
    out = x + jnp.dot(h, wout_ref[...], preferred_element_type=F32) + bout_ref[...]
    y_ref[...] = out.reshape(bb, ts, D_MODEL)
    st_ref[...] = ext_ref[:, ts:ts + CONV_HALO, :]


def _conv_layer(x, prev32, ng, win, bin_, wdw, bdw, lng, lnb, wout, bout, *, bb, ts):
    B, S, _ = x.shape
    full = lambda shape: pl.BlockSpec(shape, lambda b, s: (0,) * len(shape))
    return pl.pallas_call(
        functools.partial(_conv_kernel, ts=ts),
        grid=(B // bb, S // ts),
        in_specs=[
            pl.BlockSpec((bb, ts, D_MODEL), lambda b, s: (b, s, 0)),
            pl.BlockSpec((bb, CONV_HALO, D_MODEL), lambda b, s: (b, 0, 0)),
            full((1, D_MODEL)), full((D_MODEL, 2 * D_MODEL)), full((1, 2 * D_MODEL)),
            full((CONV_WIDTH, D_MODEL)), full((1, D_MODEL)), full((1, D_MODEL)), full((1, D_MODEL)),
            full((D_MODEL, D_MODEL)), full((1, D_MODEL)),
        ],
        out_specs=[
            pl.BlockSpec((bb, ts, D_MODEL), lambda b, s: (b, s, 0)),
            pl.BlockSpec((bb, CONV_HALO, D_MODEL), lambda b, s: (b, 0, 0)),
        ],
        out_shape=[jax.ShapeDtypeStruct((B, S, D_MODEL), F32),
                   jax.ShapeDtypeStruct((B, CONV_HALO, D_MODEL), F32)],
        scratch_shapes=[pltpu.VMEM((bb, CONV_HALO + ts, D_MODEL), F32)],
        compiler_params=_params("parallel", "arbitrary"),
        name="conv_module",
    )(x, prev32, ng, win, bin_, wdw, bdw, lng, lnb, wout, bout)


FF_SLAB = 256


def _ffn_kernel(x_ref, g_ref, wgu_ref, wd_ref, y_ref, h_ref):
    x = x_ref[...]
    xn = _rms(x, g_ref[...]).astype(BF16)
    for k in range(D_FF // FF_SLAB):
        gate = jnp.dot(xn, wgu_ref[:, k * FF_SLAB:(k + 1) * FF_SLAB], preferred_element_type=F32)
        up = jnp.dot(xn, wgu_ref[:, D_FF + k * FF_SLAB:D_FF + (k + 1) * FF_SLAB], preferred_element_type=F32)
        h_ref[:, k * FF_SLAB:(k + 1) * FF_SLAB] = (gate * _sigmoid(gate) * up).astype(BF16)
    y_ref[...] = x + jnp.dot(h_ref[...], wd_ref[...], preferred_element_type=F32)


def _ffn_layer(x, g, wgu, wd, *, tm):
    M = x.shape[0]
    resident = lambda shape: pl.BlockSpec(shape, lambda i: (0, 0), pipeline_mode=pl.Buffered(1))
    return pl.pallas_call(
        _ffn_kernel,
        grid=(M // tm,),
        in_specs=[
            pl.BlockSpec((tm, D_MODEL), lambda i: (i, 0)),
            pl.BlockSpec((1, D_MODEL), lambda i: (0, 0)),
            resident((D_MODEL, 2 * D_FF)),
            resident((D_FF, D_MODEL)),
        ],
        out_specs=pl.BlockSpec((tm, D_MODEL), lambda i: (i, 0)),
        out_shape=jax.ShapeDtypeStruct((M, D_MODEL), F32),
        scratch_shapes=[pltpu.VMEM((tm, D_FF), BF16)],
        compiler_params=_params("parallel"),
        name="swiglu_ffn",
    )(x, g, wgu, wd)


def _normed_slab(y, gain, cos, sin, e):
    ssq = jnp.dot((y * y).astype(BF16), e, preferred_element_type=F32)
    yn = y * lax.rsqrt(ssq * (1.0 / HEAD_DIM) + EPS) * gain
    lane = lax.broadcasted_iota(jnp.int32, (y.shape[0], LANES), 1)
    lo_half = (lane % HEAD_DIM) < (HEAD_DIM // 2)
    outs = []
    for k in range(MXU_N // LANES):
        z = yn[:, k * LANES:(k + 1) * LANES]
        partner = jnp.where(lo_half, pltpu.roll(z, LANES - HEAD_DIM // 2, 1), pltpu.roll(z, HEAD_DIM // 2, 1))
        outs.append(z * cos + partner * sin)
    return jnp.concatenate(outs, axis=-1)


def _proj_kernel(*refs, tm, n_chunks, plain_every, tails, n_s):
    x_ref, g_ref, w_ref, gain_ref, cos_ref, sin_ref, e_ref = refs[:7]
    ob_refs = refs[7:7 + n_chunks]
    tail_refs = refs[7 + n_chunks:7 + n_chunks + len(tails)]
    xn_ref = refs[-1]
    i = pl.program_id(0)
    c = pl.program_id(1)
    s = i % n_s

    @pl.when(c == 0)
    def _():
        xn_ref[...] = _rms(x_ref[...], g_ref[...]).astype(BF16)

    def chunk(ob_ref, normed, tail_ref, tail_len, col0):
        xn = xn_ref[...]
        for k in range(D_MODEL // PROJ_SLAB):
            y = jnp.dot(xn, w_ref[:, k * PROJ_SLAB:(k + 1) * PROJ_SLAB], preferred_element_type=F32)
            for h in range(PROJ_SLAB // MXU_N):
                sl = slice(k * PROJ_SLAB + h * MXU_N, k * PROJ_SLAB + (h + 1) * MXU_N)
                z = y[:, h * MXU_N:(h + 1) * MXU_N]
                if normed:
                    z = _normed_slab(z, gain_ref[0, :, sl], cos_ref[...], sin_ref[...], e_ref[...])
                ob_ref[:, sl] = z.astype(BF16)
                if tail_ref is not None:
                    tsl = slice(col0 + sl.start, col0 + sl.stop)
                    if tail_len >= tm:
                        @pl.when(s >= n_s - tail_len // tm)
                        def _():
                            tail_ref[0, :, tsl] = z
                    else:
                        @pl.when(s == n_s - 1)
                        def _():
                            tail_ref[0, :, tsl] = z[tm - tail_len:, :]

    for cs in range(n_chunks):
        g, part = divmod(cs, max(plain_every, 1))
        tail = (tail_refs[g], tails[g]) if tails else (None, 0)
        pl.when(c == cs)(functools.partial(chunk, ob_refs[cs], part == 0, *tail, part * D_MODEL))


def _proj(x, g, w, gain, cos, sin, e, *, tm, plain_every, seq=None, tails=()):
    M = x.shape[0]
    n_chunks = w.shape[1] // D_MODEL
    n_pos_tiles = cos.shape[0] // tm
    gdiv = max(plain_every, 1)
    seq = seq or M
    n_s = seq // tm
    out_shape = [jax.ShapeDtypeStruct((M, D_MODEL), BF16)] * n_chunks
    out_specs = [pl.BlockSpec((tm, D_MODEL), lambda i, c: (i, 0))] * n_chunks
    for lg in tails:
        out_shape.append(jax.ShapeDtypeStruct((M // seq, lg, 2 * D_MODEL), F32))
        if lg >= tm:
            first = n_s - lg // tm
            out_specs.append(pl.BlockSpec((1, tm, 2 * D_MODEL),
                                          lambda i, c, first=first: (i // n_s, jnp.maximum(i % n_s - first, 0), 0)))
        else:
            out_specs.append(pl.BlockSpec((1, lg, 2 * D_MODEL), lambda i, c: (i // n_s, 0, 0)))
    return pl.pallas_call(
        functools.partial(_proj_kernel, tm=tm, n_chunks=n_chunks, plain_every=plain_every, tails=tuple(tails), n_s=n_s),
        grid=(M // tm, n_chunks),
        in_specs=[
            pl.BlockSpec((tm, D_MODEL), lambda i, c: (i, 0)),
            pl.BlockSpec((1, D_MODEL), lambda i, c: (0, 0)),
            pl.BlockSpec((D_MODEL, D_MODEL), lambda i, c: (0, c)),
            pl.BlockSpec((1, 1, D_MODEL), lambda i, c: (c // gdiv, 0, 0)),
            pl.BlockSpec((tm, LANES), lambda i, c: (i % n_pos_tiles, 0)),
            pl.BlockSpec((tm, LANES), lambda i, c: (i % n_pos_tiles, 0)),
            pl.BlockSpec((MXU_N, MXU_N), lambda i, c: (0, 0)),
        ],
        out_specs=out_specs,
        out_shape=out_shape,
        scratch_shapes=[pltpu.VMEM((tm, D_MODEL), BF16)],
        compiler_params=_params("arbitrary", "arbitrary"),
        name="norm_proj_rotary",
    )(x, g, w, gain, cos, sin, e)


def _rope_tables(pos):
    half = HEAD_DIM // 2
    inv = ROPE_THETA ** (-jnp.arange(half, dtype=F32) / half)
    ang = pos.astype(F32)[:, None] * inv[None, :]
    cos, sin = jnp.cos(ang), jnp.sin(ang)
    reps = LANES // HEAD_DIM
    return jnp.tile(cos, (1, 2 * reps)), jnp.tile(jnp.concatenate([-sin, sin], axis=1), (1, reps))


def _qk(q, k):
    return lax.dot_general(q, k, (((1,), (1,)), ((), ())), preferred_element_type=F32)


def _head_masks(rows):
    first = lax.broadcasted_iota(jnp.int32, (rows, LANES), 1) < HEAD_DIM
    return first


def _band_attn_kernel(q_ref, k_ref, v_ref, o_ref, lse_ref, kf_ref, vf_ref, s_ref, p_ref, *, tq):
    n = pl.program_id(2)

    @pl.when(n == 0)
    def _():
        kf_ref[0:BLK, :] = jnp.zeros((BLK, D_MODEL), BF16)
        vf_ref[0:BLK, :] = jnp.zeros((BLK, D_MODEL), BF16)

    @pl.when(n > 0)
    def _():
        kf_ref[0:BLK, :] = kf_ref[tq:tq + BLK, :]
        vf_ref[0:BLK, :] = vf_ref[tq:tq + BLK, :]

    kf_ref[BLK:BLK + tq, :] = k_ref[0]
    vf_ref[BLK:BLK + tq, :] = v_ref[0]

    qi = lax.broadcasted_iota(jnp.int32, (BLK, 2 * BLK), 0)
    ki = lax.broadcasted_iota(jnp.int32, (BLK, 2 * BLK), 1)
    dist = BLK + qi - ki
    band = (dist >= 0) & (dist <= BLK)
    first = _head_masks(BLK)
    lane = lax.broadcasted_iota(jnp.int32, (BLK, LANES), 1)
    zero = jnp.zeros((BLK, LANES), BF16)

    def block(i, carry):
        r0 = pl.multiple_of(i * BLK, BLK)
        mask = band & ((ki >= BLK) | (n + i > 0))
        for p in range(N_HEADS // 2):
            sl = slice(p * LANES, (p + 1) * LANES)
            qp = q_ref[0, pl.ds(r0, BLK), sl]
            kp = kf_ref[pl.ds(r0, 2 * BLK), sl]
            s_ref[2 * p] = jnp.where(mask, _qk(jnp.where(first, qp, zero), kp), -jnp.inf)
            s_ref[2 * p + 1] = jnp.where(mask, _qk(jnp.where(first, zero, qp), kp), -jnp.inf)
        s = s_ref[...]
        m = jnp.max(s, axis=-1, keepdims=True)
        e = jnp.exp(s - m)
        l = jnp.sum(e, axis=-1, keepdims=True)
        p_ref[...] = e.astype(BF16)
        inv = 1.0 / l
        lse = m + jnp.log(l)
        lse_tile = jnp.zeros((BLK, LANES), F32)
        for p in range(N_HEADS // 2):
            sl = slice(p * LANES, (p + 1) * LANES)
            vp = vf_ref[pl.ds(r0, 2 * BLK), sl]
            oa = jnp.dot(p_ref[2 * p], vp, preferred_element_type=F32) * inv[2 * p]
            ob = jnp.dot(p_ref[2 * p + 1], vp, preferred_element_type=F32) * inv[2 * p + 1]
            o_ref[0, pl.ds(r0, BLK), sl] = jnp.where(first, oa, ob).astype(BF16)
            lse_tile = jnp.where(lane == 2 * p, lse[2 * p], lse_tile)
            lse_tile = jnp.where(lane == 2 * p + 1, lse[2 * p + 1], lse_tile)
        lse_ref[0, pl.ds(r0, BLK), :] = lse_tile
        return carry

    lax.fori_loop(0, tq // BLK, block, 0)


def _band_attention(q, k, v, g, *, B, S):
    r = DILATIONS[g]
    L = S // r
    tq = min(L, 4 * BLK)
    view = lambda a: a.reshape(B, L, r * D_MODEL)
    spec = pl.BlockSpec((1, tq, D_MODEL), lambda b, c, n: (b, n, c))
    o, lse = pl.pallas_call(
        functools.partial(_band_attn_kernel, tq=tq),
        grid=(B, r, L // tq),
        in_specs=[spec, spec, spec],
        out_specs=[
            pl.BlockSpec((1, tq, D_MODEL), lambda b, c, n: (b, n, c)),
            pl.BlockSpec((1, tq, LANES), lambda b, c, n: (b, n, c)),
        ],
        out_shape=[jax.ShapeDtypeStruct((B, L, r * D_MODEL), BF16),
                   jax.ShapeDtypeStruct((B, L, r * LANES), F32)],
        scratch_shapes=[pltpu.VMEM((BLK + tq, D_MODEL), BF16), pltpu.VMEM((BLK + tq, D_MODEL), BF16),
                        pltpu.VMEM((N_HEADS, BLK, 2 * BLK), F32), pltpu.VMEM((N_HEADS, BLK, 2 * BLK), BF16)],
        compiler_params=_params("parallel", "parallel", "arbitrary"),
        name=f"band_attention_g{g}",
    )(view(q), view(k), view(v))
    return o.reshape(B * S, D_MODEL), lse.reshape(B * S, LANES)


NEW_PAD = LANES


def _cache_attn_kernel(*refs, lb, r, t, hb, roll):
    if roll:
        q_ref, kn_ref, vn_ref, newt_ref, c_ref, o_ref, lse_ref, roll_ref, new_ref = refs
    else:
        q_ref, kn_ref, vn_ref, c_ref, o_ref, lse_ref, new_ref = refs
    j = pl.program_id(1)
    cols = hb * HEAD_DIM
    new_ref[...] = jnp.zeros((NEW_PAD, 2 * cols), BF16)
    new_ref[0:t, 0:cols] = kn_ref[0]
    new_ref[0:t, cols:2 * cols] = vn_ref[0]

    ti = lax.broadcasted_iota(jnp.int32, (t, NEW_PAD + lb), 0)
    rho = lax.broadcasted_iota(jnp.int32, (t, NEW_PAD + lb), 1)
    idx = jnp.where(rho < NEW_PAD, lb + rho, rho - NEW_PAD)
    d = lb + ti - idx
    mask = ((rho < t) | (rho >= NEW_PAD)) & (d >= 0) & (d <= BLK * r) & ((d & (r - 1)) == 0)
    first = _head_masks(t)
    lane = lax.broadcasted_iota(jnp.int32, (t, LANES), 1)
    zero = jnp.zeros((t, LANES), BF16)

    lse_tile = jnp.zeros((t, LANES), F32)
    for p in range(hb // 2):
        ks = slice(p * LANES, (p + 1) * LANES)
        vs = slice(cols + p * LANES, cols + (p + 1) * LANES)
        qp = q_ref[0, :, ks]
        k2 = c_ref[0, 0, 2 * p:2 * p + 2].reshape(LANES, lb).astype(BF16)
        v2 = c_ref[0, 1, 2 * p:2 * p + 2].reshape(LANES, lb).astype(BF16)
        outs = []
        for hh, qm in enumerate((jnp.where(first, qp, zero), jnp.where(first, zero, qp))):
            s = jnp.concatenate([_qk(qm, new_ref[:, ks]), jnp.dot(qm, k2, preferred_element_type=F32)], axis=1)
            s = jnp.where(mask, s, -jnp.inf)
            m = jnp.max(s, axis=-1, keepdims=True)
            e = jnp.exp(s - m)
            l = jnp.sum(e, axis=-1, keepdims=True)
            eb = e.astype(BF16)
            o = (jnp.dot(eb[:, :NEW_PAD], new_ref[:, vs], preferred_element_type=F32)
                 + _qk(eb[:, NEW_PAD:], v2)) * (1.0 / l)
            outs.append(o)
            lse_tile = jnp.where(lane == j * hb + 2 * p + hh, m + jnp.log(l), lse_tile)
        o_ref[0, :, ks] = jnp.where(first, outs[0], outs[1])
    lse_ref[0, 0] = lse_tile

    if roll:
        for kv in range(2):
            for h in range(hb):
                roll_ref[0, kv, h] = pltpu.roll(c_ref[0, kv, h], lb - t, 1)
                roll_ref[0, kv, h, :, lb - t:lb] = newt_ref[0, kv, h]


CACHE_BLOCK_BYTES = 8 * 1024 * 1024


def _cache_attention(q, kn, vn, cache_t, g, new_t=None):
    B, T, _ = q.shape
    lb = cache_t.shape[-1]
    r = DILATIONS[g]
    hb = N_HEADS
    while 2 * hb * HEAD_DIM * lb * 4 > CACHE_BLOCK_BYTES:
        hb //= 2
    nj = N_HEADS // hb
    roll = new_t is not None
    new = pl.BlockSpec((1, T, hb * HEAD_DIM), lambda b, j: (b, 0, j))
    heads = lambda w: pl.BlockSpec((1, 2, hb, HEAD_DIM, w), lambda b, j: (b, 0, j, 0, 0))
    out_specs = [new, pl.BlockSpec((1, 1, T, LANES), lambda b, j: (b, j, 0, 0))]
    out_shape = [jax.ShapeDtypeStruct((B, T, D_MODEL), F32), jax.ShapeDtypeStruct((B, nj, T, LANES), F32)]
    if roll:
        out_specs.append(heads(lb))
        out_shape.append(jax.ShapeDtypeStruct(cache_t.shape, cache_t.dtype))
    o, lse, *rolled = pl.pallas_call(
        functools.partial(_cache_attn_kernel, lb=lb, r=r, t=T, hb=hb, roll=roll),
        grid=(B, nj),
        in_specs=[new, new, new] + ([heads(T)] if roll else []) + [heads(lb)],
        out_specs=out_specs,
        out_shape=out_shape,
        scratch_shapes=[pltpu.VMEM((NEW_PAD, 2 * hb * HEAD_DIM), BF16)],
        compiler_params=_params("parallel", "parallel"),
        name=f"cache_attention_g{g}",
    )(q, kn, vn, *([new_t] if roll else []), cache_t)
    return o.reshape(B * T, D_MODEL), jnp.sum(lse, axis=1).reshape(B * T, LANES), rolled


def _merge_kernel(*refs, n):
    x_ref = refs[0]
    o_refs = refs[1:1 + n]
    l_refs = refs[1 + n:1 + 2 * n]
    wo_ref, y_ref = refs[1 + 2 * n:]
    ls = [r[...] for r in l_refs]
    m = functools.reduce(jnp.maximum, ls)
    es = [jnp.exp(l - m) for l in ls]
    inv = 1.0 / functools.reduce(lambda a, b: a + b, es)
    ws = [e * inv for e in es]
    first = _head_masks(x_ref.shape[0])
    cols = []
    for p in range(N_HEADS // 2):
        sl = slice(p * LANES, (p + 1) * LANES)
        acc = None
        for w, o_ref in zip(ws, o_refs):
            wp = jnp.where(first, w[:, 2 * p:2 * p + 1], w[:, 2 * p + 1:2 * p + 2])
            term = wp * o_ref[:, sl].astype(F32)
            acc = term if acc is None else acc + term
        cols.append(acc.astype(BF16))
    merged = jnp.concatenate(cols, axis=-1)
    y_ref[...] = x_ref[...] + jnp.dot(merged, wo_ref[...], preferred_element_type=F32)


def _merge_layer(x, outs, lses, wo, *, tm):
    M = x.shape[0]
    n = len(outs)
    row = lambda w: pl.BlockSpec((tm, w), lambda i: (i, 0))
    return pl.pallas_call(
        functools.partial(_merge_kernel, n=n),
        grid=(M // tm,),
        in_specs=[row(D_MODEL)] * (1 + n) + [row(LANES)] * n + [pl.BlockSpec((D_MODEL, D_MODEL), lambda i: (0, 0))],
        out_specs=row(D_MODEL),
        out_shape=jax.ShapeDtypeStruct((M, D_MODEL), F32),
        compiler_params=_params("parallel"),
        name="merge_groups_wo",
    )(x, *outs, *lses, wo)


def _seq_minor(a):
    return jnp.transpose(a, (0, 2, 3, 4, 1))


def _seq_major(a):
    return jnp.transpose(a, (0, 4, 1, 2, 3))


def _trunk(x, pos, conv_prev, kv_caches, w, *, bb, ts, tm):
    B, S, _ = x.shape
    M = B * S
    cos, sin = _rope_tables(pos)
    if S < tm:
        cos, sin = jnp.tile(cos, (tm // S, 1)), jnp.tile(sin, (tm // S, 1))
    prev32 = jnp.pad(conv_prev, ((0, 0), (0, 0), (CONV_HALO - (CONV_WIDTH - 1), 0), (0, 0)))

    conv_states = []
    new_bufs = []
    kv = None
    compact = None
    for layer in range(DEPTH):
        if layer < N_A:
            x, st = _conv_layer(x, prev32[layer], w['a_norm'][layer], w['a_w_in'][layer], w['a_b_in'][layer],
                                w['a_w_dw'][layer], w['a_b_dw'][layer], w['a_ln_g'][layer], w['a_ln_b'][layer],
                                w['a_w_out'][layer], w['a_b_out'][layer], bb=bb, ts=ts)
            conv_states.append(st[:, CONV_HALO - (CONV_WIDTH - 1):])
            xf = x.reshape(M, D_MODEL)
        else:
            j = layer - N_A
            if layer == N_A:
                if kv_caches is None:
                    seq, tails = S, tuple(min(wd, S) for wd in WINDOWS)
                else:
                    seq, tails = M, (M,) * N_GROUPS
                *kv, t0, t1, t2 = _proj(xf, w['kv_norm'], w['w_kv'], w['k_gain'], cos, sin, w['head_ones'],
                                        tm=tm, plain_every=2, seq=seq, tails=tails)
                tails_f32 = [t.reshape(B, -1, 2, N_HEADS, HEAD_DIM) for t in (t0, t1, t2)]
                if kv_caches is None:
                    new_bufs = tails_f32
            q = _proj(xf, w['b_norm'][j], w['w_q'][j], w['q_gain'][j], cos, sin, w['head_ones'],
                      tm=tm, plain_every=0)
            outs, lses = [], []
            for g in range(N_GROUPS):
                if kv_caches is None:
                    o, l = _band_attention(q[g], kv[2 * g], kv[2 * g + 1], g, B=B, S=S)
                else:
                    new = lambda a: a.reshape(B, S, D_MODEL)
                    new_t = _seq_minor(tails_f32[g]) if layer == N_A else None
                    o, l, rolled = _cache_attention(new(q[g]), new(kv[2 * g]), new(kv[2 * g + 1]),
                                                    _seq_minor(kv_caches[g]), g, new_t)
                    if rolled:
                        new_bufs.append(_seq_major(rolled[0]))
                outs.append(o)
                lses.append(l)
            xf = _merge_layer(xf, outs, lses, w['w_o'][j], tm=tm)
        xf = _ffn_layer(xf, w['ffn_norm'][layer], w['w_gate_up'][layer], w['w_down'][layer], tm=tm)
        x = xf.reshape(B, S, D_MODEL)
    return x, jnp.stack(conv_states, axis=0), new_bufs


def kernel(x_prompt, x_sample, cache_conv, cache_kv_w128, cache_kv_w512, cache_kv_w2048,
           a_norm, a_w_in, a_b_in, a_w_dw, a_b_dw, a_ln_g, a_ln_b, a_w_out, a_b_out,
           kv_norm, w_kv, k_norm, b_norm, w_q, q_norm, w_o, ffn_norm, w_gate_up, w_down):
    B, S, _ = x_prompt.shape
    Bd, T, _ = x_sample.shape
    row = lambda a: a.reshape(a.shape[:-1] + (1, a.shape[-1]))
    head_of = jnp.arange(MXU_N, dtype=jnp.int32) // HEAD_DIM
    w = dict(
        a_norm=row(a_norm), a_w_in=a_w_in.astype(BF16), a_b_in=row(a_b_in), a_w_dw=a_w_dw, a_b_dw=row(a_b_dw),
        a_ln_g=row(a_ln_g), a_ln_b=row(a_ln_b), a_w_out=a_w_out.astype(BF16), a_b_out=row(a_b_out),
        kv_norm=row(kv_norm), w_kv=w_kv.astype(BF16), b_norm=row(b_norm), w_q=w_q.astype(BF16),
        w_o=w_o.astype(BF16), ffn_norm=row(ffn_norm), w_gate_up=w_gate_up.astype(BF16), w_down=w_down.astype(BF16),
        k_gain=row(jnp.tile(k_norm, (1, N_HEADS))),
        q_gain=row(jnp.tile(q_norm, (1, 1, N_HEADS)) * SCALE),
        head_ones=(head_of[:, None] == head_of[None, :]).astype(BF16),
    )
    pos_prompt = jnp.arange(S, dtype=jnp.int32)
    pos_sample = PAST_LEN + jnp.arange(T, dtype=jnp.int32)
    conv_zero = jnp.zeros((N_A, B, CONV_WIDTH - 1, D_MODEL), x_prompt.dtype)
    y_p, conv_p, bufs_p = _trunk(x_prompt, pos_prompt, conv_zero, None, w, bb=1, ts=512, tm=512)
    y_s, conv_s, bufs_s = _trunk(x_sample, pos_sample, cache_conv,
                                 (cache_kv_w128, cache_kv_w512, cache_kv_w2048), w, bb=Bd, ts=T, tm=Bd * T)
    return (y_p, y_s, conv_p, conv_s, bufs_p[0], bufs_p[1], bufs_p[2], bufs_s[0], bufs_s[1], bufs_s[2])
```

```python
import functools

import jax
import jax.numpy as jnp
from jax import lax
from jax.experimental import pallas as pl
from jax.experimental.pallas import tpu as pltpu

D_MODEL = 1024
DEPTH = 4
PAST_LEN = 8192
N_A = DEPTH // 2
N_B = DEPTH - N_A
CONV_WIDTH = 31
CONV_HALO = 32
SUBLANES = 8
LANES = 128
N_HEADS = 16
HEAD_DIM = D_MODEL // N_HEADS
WINDOWS = (128, 512, 2048)
DILATIONS = (1, 4, 16)
N_GROUPS = len(WINDOWS)
BLK = 128
D_FF = (8 * D_MODEL + 3 * 256 - 1) // (3 * 256) * 256
ROPE_THETA = 10000.0
EPS = 1e-6
SCALE = HEAD_DIM ** -0.5
MXU_N = 256
PROJ_SLAB = 2 * MXU_N
VMEM_LIMIT = 56 * 1024 * 1024

F32 = jnp.float32
BF16 = jnp.bfloat16


def _params(*sem):
    return pltpu.CompilerParams(dimension_semantics=sem, vmem_limit_bytes=VMEM_LIMIT)


def _rms(x, g):
    return x * lax.rsqrt(jnp.mean(x * x, axis=-1, keepdims=True) + EPS) * g


def _sigmoid(x):
    return 1.0 / (1.0 + jnp.exp(-x))


def _conv_kernel(x_ref, prev_ref, ng_ref, win_ref, bin_ref, wdw_ref, bdw_ref, lng_ref, lnb_ref,
                 wout_ref, bout_ref, y_ref, st_ref, ext_ref, *, ts):
    s = pl.program_id(1)
    bb = x_ref.shape[0]

    @pl.when(s == 0)
    def _():
        ext_ref[:, 0:CONV_HALO, :] = prev_ref[...]

    @pl.when(s > 0)
    def _():
        ext_ref[:, 0:CONV_HALO, :] = ext_ref[:, ts:ts + CONV_HALO, :]

    x = x_ref[...].reshape(bb * ts, D_MODEL)
    u = _rms(x, ng_ref[...]).astype(BF16)
    a = jnp.dot(u, win_ref[...], preferred_element_type=F32) + bin_ref[...]
    glu = a[:, :D_MODEL] * _sigmoid(a[:, D_MODEL:])
    ext_ref[:, CONV_HALO:CONV_HALO + ts, :] = glu.reshape(bb, ts, D_MODEL)

    off = CONV_HALO - (CONV_WIDTH - 1)
    c = None
    for b in range(SUBLANES):
        rows = ts if b == 0 else ts + SUBLANES
        yb = None
        for a_ in range((off + CONV_WIDTH - 1) // SUBLANES + 1):
            j = SUBLANES * a_ + b - off
            if 0 <= j < CONV_WIDTH:
                term = ext_ref[:, SUBLANES * a_:SUBLANES * a_ + rows, :] * wdw_ref[j:j + 1, :]
                yb = term if yb is None else yb + term
        yb = yb[:, b:b + ts, :]
        c = yb if c is None else c + yb
    c = c.reshape(bb * ts, D_MODEL) + bdw_ref[...]
    mu = jnp.mean(c, axis=-1, keepdims=True)
    cc = c - mu
    yn = cc * lax.rsqrt(jnp.mean(cc * cc, axis=-1, keepdims=True) + EPS) * lng_ref[...] + lnb_ref[...]
    h = (yn * _sigmoid(yn)).astype(BF16)
    out = x + jnp.dot(h, wout_ref[...], preferred_element_type=F32) + bout_ref[...]
    y_ref[...] = out.reshape(bb, ts, D_MODEL)
    st_ref[...] = ext_ref[:, ts:ts + CONV_HALO, :]


def _conv_layer(x, prev32, ng, win, bin_, wdw, bdw, lng, lnb, wout, bout, *, bb, ts):
    B, S, _ = x.shape
    full = lambda shape: pl.BlockSpec(shape, lambda b, s: (0,) * len(shape))
    return pl.pallas_call(
        functools.partial(_conv_kernel, ts=ts),
        grid=(B // bb, S // ts),
        in_specs=[
            pl.BlockSpec((bb, ts, D_MODEL), lambda b, s: (b, s, 0)),
            pl.BlockSpec((bb, CONV_HALO, D_MODEL), lambda b, s: (b, 0, 0)),
            full((1, D_MODEL)), full((D_MODEL, 2 * D_MODEL)), full((1, 2 * D_MODEL)),
            full((CONV_WIDTH, D_MODEL)), full((1, D_MODEL)), full((1, D_MODEL)), full((1, D_MODEL)),
            full((D_MODEL, D_MODEL)), full((1, D_MODEL)),
        ],
        out_specs=[
            pl.BlockSpec((bb, ts, D_MODEL), lambda b, s: (b, s, 0)),
            pl.BlockSpec((bb, CONV_HALO, D_MODEL), lambda b, s: (b, 0, 0)),
        ],
        out_shape=[jax.ShapeDtypeStruct((B, S, D_MODEL), F32),
                   jax.ShapeDtypeStruct((B, CONV_HALO, D_MODEL), F32)],
        scratch_shapes=[pltpu.VMEM((bb, CONV_HALO + ts, D_MODEL), F32)],
        compiler_params=_params("parallel", "arbitrary"),
        name="conv_module",
    )(x, prev32, ng, win, bin_, wdw, bdw, lng, lnb, wout, bout)


FF_SLAB = 256


def _ffn_kernel(x_ref, g_ref, wgu_ref, wd_ref, y_ref, h_ref):
    x = x_ref[...]
    xn = _rms(x, g_ref[...]).astype(BF16)
    for k in range(D_FF // FF_SLAB):
        gate = jnp.dot(xn, wgu_ref[:, k * FF_SLAB:(k + 1) * FF_SLAB], preferred_element_type=F32)
        up = jnp.dot(xn, wgu_ref[:, D_FF + k * FF_SLAB:D_FF + (k + 1) * FF_SLAB], preferred_element_type=F32)
        h_ref[:, k * FF_SLAB:(k + 1) * FF_SLAB] = (gate * _sigmoid(gate) * up).astype(BF16)
    y_ref[...] = x + jnp.dot(h_ref[...], wd_ref[...], preferred_element_type=F32)


def _ffn_layer(x, g, wgu, wd, *, tm):
    M = x.shape[0]
    resident = lambda shape: pl.BlockSpec(shape, lambda i: (0, 0), pipeline_mode=pl.Buffered(1))
    return pl.pallas_call(
        _ffn_kernel,
        grid=(M // tm,),
        in_specs=[
            pl.BlockSpec((tm, D_MODEL), lambda i: (i, 0)),
            pl.BlockSpec((1, D_MODEL), lambda i: (0, 0)),
            resident((D_MODEL, 2 * D_FF)),
            resident((D_FF, D_MODEL)),
        ],
        out_specs=pl.BlockSpec((tm, D_MODEL), lambda i: (i, 0)),
        out_shape=jax.ShapeDtypeStruct((M, D_MODEL), F32),
        scratch_shapes=[pltpu.VMEM((tm, D_FF), BF16)],
        compiler_params=_params("parallel"),
        name="swiglu_ffn",
    )(x, g, wgu, wd)


def _normed_slab(y, gain, cos, sin, e):
    ssq = jnp.dot((y * y).astype(BF16), e, preferred_element_type=F32)
    yn = y * lax.rsqrt(ssq * (1.0 / HEAD_DIM) + EPS) * gain
    lane = lax.broadcasted_iota(jnp.int32, (y.shape[0], LANES), 1)
    lo_half = (lane % HEAD_DIM) < (HEAD_DIM // 2)
    outs = []
    for k in range(MXU_N // LANES):
        z = yn[:, k * LANES:(k + 1) * LANES]
        partner = jnp.where(lo_half, pltpu.roll(z, LANES - HEAD_DIM // 2, 1), pltpu.roll(z, HEAD_DIM // 2, 1))
        outs.append(z * cos + partner * sin)
    return jnp.concatenate(outs, axis=-1)


def _proj_kernel(*refs, tm, n_chunks, plain_every, tails, n_s, rs):
    x_ref, g_ref, w_ref, gain_ref, cos_ref, sin_ref, e_ref = refs[:7]
    n_perm = sum(r > 1 for r in rs)
    perm_refs = refs[7:7 + n_perm]
    ob_refs = refs[7 + n_perm:7 + n_perm + n_chunks]
    tail_refs = refs[7 + n_perm + n_chunks:7 + n_perm + n_chunks + len(tails)]
    xn_ref = refs[-1]
    i = pl.program_id(0)
    c = pl.program_id(1)
    s = i % n_s

    @pl.when(c == 0)
    def _():
        xn = _rms(x_ref[...], g_ref[...]).astype(BF16)
        k = 0
        for g, r in enumerate(rs):
            if r == 1:
                xn_ref[g] = xn
            else:
                xn_ref[g] = jnp.dot(perm_refs[k][...], xn, preferred_element_type=F32).astype(BF16)
                k += 1

    def chunk(g, ob_ref, normed, tail_ref, tail_len, col0):
        xn = xn_ref[g]
        rows = tm // rs[g]
        for k in range(D_MODEL // PROJ_SLAB):
            y = jnp.dot(xn, w_ref[:, k * PROJ_SLAB:(k + 1) * PROJ_SLAB], preferred_element_type=F32)
            for h in range(PROJ_SLAB // MXU_N):
                sl = slice(k * PROJ_SLAB + h * MXU_N, k * PROJ_SLAB + (h + 1) * MXU_N)
                z = y[:, h * MXU_N:(h + 1) * MXU_N]
                if normed:
                    z = _normed_slab(z, gain_ref[0, :, sl], cos_ref[0], sin_ref[0], e_ref[...])
                zb = z.astype(BF16)
                for cc in range(rs[g]):
                    ob_ref[0, cc, 0, :, sl] = zb[cc * rows:(cc + 1) * rows]
                if tail_ref is not None:
                    tsl = slice(col0 + sl.start, col0 + sl.stop)
                    if tail_len >= tm:
                        @pl.when(s >= n_s - tail_len // tm)
                        def _():
                            tail_ref[0, :, tsl] = z
                    else:
                        @pl.when(s == n_s - 1)
                        def _():
                            tail_ref[0, :, tsl] = z[tm - tail_len:, :]

    for cs in range(n_chunks):
        g, part = divmod(cs, max(plain_every, 1))
        tail = (tail_refs[g], tails[g]) if tails else (None, 0)
        pl.when(c == cs)(functools.partial(chunk, g, ob_refs[cs], part == 0, *tail, part * D_MODEL))


def _residue_major(n, r):
    pos = jnp.arange(n, dtype=jnp.int32)
    return (pos % (n // r)) * r + pos // (n // r)


def _proj(x, g, w, gain, cos, sin, e, *, tm, plain_every, rs, seq=None, tails=()):
    M = x.shape[0]
    n_chunks = w.shape[1] // D_MODEL
    gdiv = max(plain_every, 1)
    seq = seq or M
    n_s = seq // tm
    n_pos_tiles = cos.shape[0] // tm
    orders = [_residue_major(tm, r) for r in rs]
    perms = [(o[:, None] == jnp.arange(tm, dtype=jnp.int32)[None, :]).astype(BF16) for o, r in zip(orders, rs) if r > 1]
    tile_order = lambda tab, o: tab.reshape(n_pos_tiles, tm, LANES)[:, o].reshape(n_pos_tiles * tm, LANES)
    cos = jnp.stack([tile_order(cos, o) for o in orders])
    sin = jnp.stack([tile_order(sin, o) for o in orders])
    out_shape, out_specs = [], []
    for cs in range(n_chunks):
        r = rs[cs // gdiv]
        out_shape.append(jax.ShapeDtypeStruct((M // seq, r, n_s, tm // r, D_MODEL), BF16))
        out_specs.append(pl.BlockSpec((1, r, 1, tm // r, D_MODEL), lambda i, c: (i // n_s, 0, i % n_s, 0, 0)))
    for lg in tails:
        out_shape.append(jax.ShapeDtypeStruct((M // seq, lg, 2 * D_MODEL), F32))
        if lg >= tm:
            first = n_s - lg // tm
            out_specs.append(pl.BlockSpec((1, tm, 2 * D_MODEL),
                                          lambda i, c, first=first: (i // n_s, jnp.maximum(i % n_s - first, 0), 0)))
        else:
            out_specs.append(pl.BlockSpec((1, lg, 2 * D_MODEL), lambda i, c: (i // n_s, 0, 0)))
    table = pl.BlockSpec((1, tm, LANES), lambda i, c: (c // gdiv, i % n_pos_tiles, 0))
    outs = pl.pallas_call(
        functools.partial(_proj_kernel, tm=tm, n_chunks=n_chunks, plain_every=plain_every, tails=tuple(tails),
                          n_s=n_s, rs=tuple(rs)),
        grid=(M // tm, n_chunks),
        in_specs=[
            pl.BlockSpec((tm, D_MODEL), lambda i, c: (i, 0)),
            pl.BlockSpec((1, D_MODEL), lambda i, c: (0, 0)),
            pl.BlockSpec((D_MODEL, D_MODEL), lambda i, c: (0, c)),
            pl.BlockSpec((1, 1, D_MODEL), lambda i, c: (c // gdiv, 0, 0)),
            table, table,
            pl.BlockSpec((MXU_N, MXU_N), lambda i, c: (0, 0)),
        ] + [pl.BlockSpec((tm, tm), lambda i, c: (0, 0))] * len(perms),
        out_specs=out_specs,
        out_shape=out_shape,
        scratch_shapes=[pltpu.VMEM((len(rs), tm, D_MODEL), BF16)],
        compiler_params=_params("arbitrary", "arbitrary"),
        name="norm_proj_rotary",
    )(x, g, w, gain, cos, sin, e, *perms)
    chunks = [o.reshape(M // seq * rs[cs // gdiv], seq // rs[cs // gdiv], D_MODEL) for cs, o in enumerate(outs[:n_chunks])]
    return chunks + list(outs[n_chunks:])


def _rope_tables(pos):
    half = HEAD_DIM // 2
    inv = ROPE_THETA ** (-jnp.arange(half, dtype=F32) / half)
    ang = pos.astype(F32)[:, None] * inv[None, :]
    cos, sin = jnp.cos(ang), jnp.sin(ang)
    reps = LANES // HEAD_DIM
    return jnp.tile(cos, (1, 2 * reps)), jnp.tile(jnp.concatenate([-sin, sin], axis=1), (1, reps))


def _qk(q, k):
    return lax.dot_general(q, k, (((1,), (1,)), ((), ())), preferred_element_type=F32)


def _head_masks(rows):
    first = lax.broadcasted_iota(jnp.int32, (rows, LANES), 1) < HEAD_DIM
    return first


def _band_attn_kernel(q_ref, k_ref, v_ref, o_ref, lse_ref, kf_ref, vf_ref, s_ref, p_ref, *, tq):
    n = pl.program_id(2)
    rows = o_ref.shape[3]

    def store(ref, i, sl, val):
        if rows >= BLK:
            per = rows // BLK
            ref[0, i // per, 0, pl.ds(pl.multiple_of((i % per) * BLK, BLK), BLK), sl] = val
        else:
            for u in range(BLK // rows):
                ref[0, i * (BLK // rows) + u, 0, :, sl] = val[u * rows:(u + 1) * rows]

    @pl.when(n == 0)
    def _():
        kf_ref[0:BLK, :] = jnp.zeros((BLK, D_MODEL), BF16)
        vf_ref[0:BLK, :] = jnp.zeros((BLK, D_MODEL), BF16)

    @pl.when(n > 0)
    def _():
        kf_ref[0:BLK, :] = kf_ref[tq:tq + BLK, :]
        vf_ref[0:BLK, :] = vf_ref[tq:tq + BLK, :]

    kf_ref[BLK:BLK + tq, :] = k_ref[0]
    vf_ref[BLK:BLK + tq, :] = v_ref[0]

    qi = lax.broadcasted_iota(jnp.int32, (BLK, 2 * BLK), 0)
    ki = lax.broadcasted_iota(jnp.int32, (BLK, 2 * BLK), 1)
    dist = BLK + qi - ki
    band = (dist >= 0) & (dist <= BLK)
    first = _head_masks(BLK)
    lane = lax.broadcasted_iota(jnp.int32, (BLK, LANES), 1)
    zero = jnp.zeros((BLK, LANES), BF16)

    def block(i, carry):
        r0 = pl.multiple_of(i * BLK, BLK)
        mask = band & ((ki >= BLK) | (n + i > 0))
        for p in range(N_HEADS // 2):
            sl = slice(p * LANES, (p + 1) * LANES)
            qp = q_ref[0, pl.ds(r0, BLK), sl]
            kp = kf_ref[pl.ds(r0, 2 * BLK), sl]
            s_ref[2 * p] = jnp.where(mask, _qk(jnp.where(first, qp, zero), kp), -jnp.inf)
            s_ref[2 * p + 1] = jnp.where(mask, _qk(jnp.where(first, zero, qp), kp), -jnp.inf)
        s = s_ref[...]
        m = jnp.max(s, axis=-1, keepdims=True)
        e = jnp.exp(s - m)
        l = jnp.sum(e, axis=-1, keepdims=True)
        p_ref[...] = e.astype(BF16)
        inv = 1.0 / l
        lse = m + jnp.log(l)
        lse_tile = jnp.zeros((BLK, LANES), F32)
        for p in range(N_HEADS // 2):
            sl = slice(p * LANES, (p + 1) * LANES)
            vp = vf_ref[pl.ds(r0, 2 * BLK), sl]
            oa = jnp.dot(p_ref[2 * p], vp, preferred_element_type=F32) * inv[2 * p]
            ob = jnp.dot(p_ref[2 * p + 1], vp, preferred_element_type=F32) * inv[2 * p + 1]
            store(o_ref, i, sl, jnp.where(first, oa, ob).astype(BF16))
            lse_tile = jnp.where(lane == 2 * p, lse[2 * p], lse_tile)
            lse_tile = jnp.where(lane == 2 * p + 1, lse[2 * p + 1], lse_tile)
        store(lse_ref, i, slice(None), lse_tile)
        return carry

    lax.fori_loop(0, tq // BLK, block, 0)


def _band_attention(q, k, v, g, *, B, S, tm):
    r = DILATIONS[g]
    L = S // r
    tq = min(L, 4 * BLK)
    rows = tm // r
    nt = tq // rows
    spec = pl.BlockSpec((1, tq, D_MODEL), lambda b, c, n: (b * r + c, n, 0))
    o, lse = pl.pallas_call(
        functools.partial(_band_attn_kernel, tq=tq),
        grid=(B, r, L // tq),
        in_specs=[spec, spec, spec],
        out_specs=[
            pl.BlockSpec((1, nt, 1, rows, D_MODEL), lambda b, c, n: (b, n, c, 0, 0)),
            pl.BlockSpec((1, nt, 1, rows, LANES), lambda b, c, n: (b, n, c, 0, 0)),
        ],
        out_shape=[jax.ShapeDtypeStruct((B, S // tm, r, rows, D_MODEL), BF16),
                   jax.ShapeDtypeStruct((B, S // tm, r, rows, LANES), F32)],
        scratch_shapes=[pltpu.VMEM((BLK + tq, D_MODEL), BF16), pltpu.VMEM((BLK + tq, D_MODEL), BF16),
                        pltpu.VMEM((N_HEADS, BLK, 2 * BLK), F32), pltpu.VMEM((N_HEADS, BLK, 2 * BLK), BF16)],
        compiler_params=_params("parallel", "parallel", "arbitrary"),
        name=f"band_attention_g{g}",
    )(q, k, v)
    return o.reshape(B * S // tm, r, rows, D_MODEL), lse.reshape(B * S // tm, r, rows, LANES)


NEW_PAD = LANES


def _cache_attn_kernel(*refs, lb, r, t, hb, roll):
    if roll:
        q_ref, kn_ref, vn_ref, newt_ref, c_ref, o_ref, lse_ref, roll_ref, new_ref = refs
    else:
        q_ref, kn_ref, vn_ref, c_ref, o_ref, lse_ref, new_ref = refs
    j = pl.program_id(1)
    cols = hb * HEAD_DIM
    new_ref[...] = jnp.zeros((NEW_PAD, 2 * cols), BF16)
    new_ref[0:t, 0:cols] = kn_ref[0]
    new_ref[0:t, cols:2 * cols] = vn_ref[0]

    ti = lax.broadcasted_iota(jnp.int32, (t, NEW_PAD + lb), 0)
    rho = lax.broadcasted_iota(jnp.int32, (t, NEW_PAD + lb), 1)
    idx = jnp.where(rho < NEW_PAD, lb + rho, rho - NEW_PAD)
    d = lb + ti - idx
    mask = ((rho < t) | (rho >= NEW_PAD)) & (d >= 0) & (d <= BLK * r) & ((d & (r - 1)) == 0)
    first = _head_masks(t)
    lane = lax.broadcasted_iota(jnp.int32, (t, LANES), 1)
    zero = jnp.zeros((t, LANES), BF16)

    lse_tile = jnp.zeros((t, LANES), F32)
    for p in range(hb // 2):
        ks = slice(p * LANES, (p + 1) * LANES)
        vs = slice(cols + p * LANES, cols + (p + 1) * LANES)
        qp = q_ref[0, :, ks]
        k2 = c_ref[0, 0, 2 * p:2 * p + 2].reshape(LANES, lb).astype(BF16)
        v2 = c_ref[0, 1, 2 * p:2 * p + 2].reshape(LANES, lb).astype(BF16)
        outs = []
        for hh, qm in enumerate((jnp.where(first, qp, zero), jnp.where(first, zero, qp))):
            s = jnp.concatenate([_qk(qm, new_ref[:, ks]), jnp.dot(qm, k2, preferred_element_type=F32)], axis=1)
            s = jnp.where(mask, s, -jnp.inf)
            m = jnp.max(s, axis=-1, keepdims=True)
            e = jnp.exp(s - m)
            l = jnp.sum(e, axis=-1, keepdims=True)
            eb = e.astype(BF16)
            o = (jnp.dot(eb[:, :NEW_PAD], new_ref[:, vs], preferred_element_type=F32)
                 + _qk(eb[:, NEW_PAD:], v2)) * (1.0 / l)
            outs.append(o)
            lse_tile = jnp.where(lane == j * hb + 2 * p + hh, m + jnp.log(l), lse_tile)
        o_ref[0, :, ks] = jnp.where(first, outs[0], outs[1])
    lse_ref[0, 0] = lse_tile

    if roll:
        for kv in range(2):
            for h in range(hb):
                roll_ref[0, kv, h] = pltpu.roll(c_ref[0, kv, h], lb - t, 1)
                roll_ref[0, kv, h, :, lb - t:lb] = newt_ref[0, kv, h]


CACHE_BLOCK_BYTES = 8 * 1024 * 1024


def _cache_attention(q, kn, vn, cache_t, g, new_t=None):
    B, T, _ = q.shape
    lb = cache_t.shape[-1]
    r = DILATIONS[g]
    hb = N_HEADS
    while 2 * hb * HEAD_DIM * lb * 4 > CACHE_BLOCK_BYTES:
        hb //= 2
    nj = N_HEADS // hb
    roll = new_t is not None
    new = pl.BlockSpec((1, T, hb * HEAD_DIM), lambda b, j: (b, 0, j))
    heads = lambda w: pl.BlockSpec((1, 2, hb, HEAD_DIM, w), lambda b, j: (b, 0, j, 0, 0))
    out_specs = [new, pl.BlockSpec((1, 1, T, LANES), lambda b, j: (b, j, 0, 0))]
    out_shape = [jax.ShapeDtypeStruct((B, T, D_MODEL), F32), jax.ShapeDtypeStruct((B, nj, T, LANES), F32)]
    if roll:
        out_specs.append(heads(lb))
        out_shape.append(jax.ShapeDtypeStruct(cache_t.shape, cache_t.dtype))
    o, lse, *rolled = pl.pallas_call(
        functools.partial(_cache_attn_kernel, lb=lb, r=r, t=T, hb=hb, roll=roll),
        grid=(B, nj),
        in_specs=[new, new, new] + ([heads(T)] if roll else []) + [heads(lb)],
        out_specs=out_specs,
        out_shape=out_shape,
        scratch_shapes=[pltpu.VMEM((NEW_PAD, 2 * hb * HEAD_DIM), BF16)],
        compiler_params=_params("parallel", "parallel"),
        name=f"cache_attention_g{g}",
    )(q, kn, vn, *([new_t] if roll else []), cache_t)
    return o.reshape(B * T, D_MODEL), jnp.sum(lse, axis=1).reshape(B * T, LANES), rolled


def _in_order(pt, v):
    if v.dtype == BF16:
        return jnp.dot(pt, v, preferred_element_type=F32)
    hi = v.astype(BF16)
    rest = v - hi.astype(F32)
    mid = rest.astype(BF16)
    lo = (rest - mid.astype(F32)).astype(BF16)
    return (jnp.dot(pt, hi, preferred_element_type=F32) + jnp.dot(pt, mid, preferred_element_type=F32)
            + jnp.dot(pt, lo, preferred_element_type=F32))


def _merge_kernel(*refs, rs):
    n = len(rs)
    x_ref = refs[0]
    o_refs = refs[1:1 + n]
    l_refs = refs[1 + n:1 + 2 * n]
    pt_refs = refs[1 + 2 * n:-2]
    wo_ref, y_ref = refs[-2:]
    tm = x_ref.shape[0]
    os_, ls = [], []
    k = 0
    for o_ref, l_ref, r in zip(o_refs, l_refs, rs):
        o = o_ref[0].reshape(tm, D_MODEL)
        l = l_ref[0].reshape(tm, LANES)
        if r > 1:
            o, l = _in_order(pt_refs[k][...], o), _in_order(pt_refs[k][...], l)
            k += 1
        os_.append(o)
        ls.append(l)
    m = functools.reduce(jnp.maximum, ls)
    es = [jnp.exp(l - m) for l in ls]
    inv = 1.0 / functools.reduce(lambda a, b: a + b, es)
    ws = [e * inv for e in es]
    first = _head_masks(tm)
    cols = []
    for p in range(N_HEADS // 2):
        sl = slice(p * LANES, (p + 1) * LANES)
        acc = None
        for w, o in zip(ws, os_):
            wp = jnp.where(first, w[:, 2 * p:2 * p + 1], w[:, 2 * p + 1:2 * p + 2])
            term = wp * o[:, sl].astype(F32)
            acc = term if acc is None else acc + term
        cols.append(acc.astype(BF16))
    merged = jnp.concatenate(cols, axis=-1)
    y_ref[...] = x_ref[...] + jnp.dot(merged, wo_ref[...], preferred_element_type=F32)


def _merge_layer(x, outs, lses, wo, *, tm, rs):
    M = x.shape[0]
    row = lambda w: pl.BlockSpec((tm, w), lambda i: (i, 0))
    part = lambda r, w: pl.BlockSpec((1, r, tm // r, w), lambda i: (i, 0, 0, 0))
    n_of = jnp.arange(tm, dtype=jnp.int32)
    pts = [(n_of[:, None] == _residue_major(tm, r)[None, :]).astype(BF16) for r in rs if r > 1]
    return pl.pallas_call(
        functools.partial(_merge_kernel, rs=tuple(rs)),
        grid=(M // tm,),
        in_specs=([row(D_MODEL)] + [part(r, D_MODEL) for r in rs] + [part(r, LANES) for r in rs]
                  + [pl.BlockSpec((tm, tm), lambda i: (0, 0))] * len(pts)
                  + [pl.BlockSpec((D_MODEL, D_MODEL), lambda i: (0, 0))]),
        out_specs=row(D_MODEL),
        out_shape=jax.ShapeDtypeStruct((M, D_MODEL), F32),
        compiler_params=_params("parallel"),
        name="merge_groups_wo",
    )(x, *outs, *lses, *pts, wo)


def _seq_minor(a):
    return jnp.transpose(a, (0, 2, 3, 4, 1))


def _seq_major(a):
    return jnp.transpose(a, (0, 4, 1, 2, 3))


def _trunk(x, pos, conv_prev, kv_caches, w, *, bb, ts, tm):
    B, S, _ = x.shape
    M = B * S
    cos, sin = _rope_tables(pos)
    if S < tm:
        cos, sin = jnp.tile(cos, (tm // S, 1)), jnp.tile(sin, (tm // S, 1))
    prev32 = jnp.pad(conv_prev, ((0, 0), (0, 0), (CONV_HALO - (CONV_WIDTH - 1), 0), (0, 0)))

    rs = DILATIONS if kv_caches is None else (1,) * N_GROUPS
    seq = S if kv_caches is None else M
    conv_states = []
    new_bufs = []
    kv = None
    for layer in range(DEPTH):
        if layer < N_A:
            x, st = _conv_layer(x, prev32[layer], w['a_norm'][layer], w['a_w_in'][layer], w['a_b_in'][layer],
                                w['a_w_dw'][layer], w['a_b_dw'][layer], w['a_ln_g'][layer], w['a_ln_b'][layer],
                                w['a_w_out'][layer], w['a_b_out'][layer], bb=bb, ts=ts)
            conv_states.append(st[:, CONV_HALO - (CONV_WIDTH - 1):])
            xf = x.reshape(M, D_MODEL)
        else:
            j = layer - N_A
            if layer == N_A:
                tails = tuple(min(wd, S) for wd in WINDOWS) if kv_caches is None else (M,) * N_GROUPS
                *kv, t0, t1, t2 = _proj(xf, w['kv_norm'], w['w_kv'], w['k_gain'], cos, sin, w['head_ones'],
                                        tm=tm, plain_every=2, rs=rs, seq=seq, tails=tails)
                tails_f32 = []
                for t, r in zip((t0, t1, t2), rs):
                    if r > 1:
                        nb, lg, C = t.shape
                        t = t.reshape(nb, lg // tm, r, tm // r, C).transpose(0, 1, 3, 2, 4)
                    tails_f32.append(t.reshape(B, -1, 2, N_HEADS, HEAD_DIM))
                if kv_caches is None:
                    new_bufs = tails_f32
            q = _proj(xf, w['b_norm'][j], w['w_q'][j], w['q_gain'][j], cos, sin, w['head_ones'],
                      tm=tm, plain_every=0, rs=rs, seq=seq)
            outs, lses = [], []
            for g in range(N_GROUPS):
                if kv_caches is None:
                    o, l = _band_attention(q[g], kv[2 * g], kv[2 * g + 1], g, B=B, S=S, tm=tm)
                else:
                    new = lambda a: a.reshape(B, S, D_MODEL)
                    new_t = _seq_minor(tails_f32[g]) if layer == N_A else None
                    o, l, rolled = _cache_attention(new(q[g]), new(kv[2 * g]), new(kv[2 * g + 1]),
                                                    _seq_minor(kv_caches[g]), g, new_t)
                    if rolled:
                        new_bufs.append(_seq_major(rolled[0]))
                    o, l = o.reshape(M // tm, 1, tm, D_MODEL), l.reshape(M // tm, 1, tm, LANES)
                outs.append(o)
                lses.append(l)
            xf = _merge_layer(xf, outs, lses, w['w_o'][j], tm=tm, rs=rs)
        xf = _ffn_layer(xf, w['ffn_norm'][layer], w['w_gate_up'][layer], w['w_down'][layer], tm=tm)
        x = xf.reshape(B, S, D_MODEL)
    return x, jnp.stack(conv_states, axis=0), new_bufs


def kernel(x_prompt, x_sample, cache_conv, cache_kv_w128, cache_kv_w512, cache_kv_w2048,
           a_norm, a_w_in, a_b_in, a_w_dw, a_b_dw, a_ln_g, a_ln_b, a_w_out, a_b_out,
           kv_norm, w_kv, k_norm, b_norm, w_q, q_norm, w_o, ffn_norm, w_gate_up, w_down):
    B, S, _ = x_prompt.shape
    Bd, T, _ = x_sample.shape
    row = lambda a: a.reshape(a.shape[:-1] + (1, a.shape[-1]))
    head_of = jnp.arange(MXU_N, dtype=jnp.int32) // HEAD_DIM
    w = dict(
        a_norm=row(a_norm), a_w_in=a_w_in.astype(BF16), a_b_in=row(a_b_in), a_w_dw=a_w_dw, a_b_dw=row(a_b_dw),
        a_ln_g=row(a_ln_g), a_ln_b=row(a_ln_b), a_w_out=a_w_out.astype(BF16), a_b_out=row(a_b_out),
        kv_norm=row(kv_norm), w_kv=w_kv.astype(BF16), b_norm=row(b_norm), w_q=w_q.astype(BF16),
        w_o=w_o.astype(BF16), ffn_norm=row(ffn_norm), w_gate_up=w_gate_up.astype(BF16), w_down=w_down.astype(BF16),
        k_gain=row(jnp.tile(k_norm, (1, N_HEADS))),
        q_gain=row(jnp.tile(q_norm, (1, 1, N_HEADS)) * SCALE),
        head_ones=(head_of[:, None] == head_of[None, :]).astype(BF16),
    )
    pos_prompt = jnp.arange(S, dtype=jnp.int32)
    pos_sample = PAST_LEN + jnp.arange(T, dtype=jnp.int32)
    conv_zero = jnp.zeros((N_A, B, CONV_WIDTH - 1, D_MODEL), x_prompt.dtype)
    y_p, conv_p, bufs_p = _trunk(x_prompt, pos_prompt, conv_zero, None, w, bb=1, ts=512, tm=512)
    y_s, conv_s, bufs_s = _trunk(x_sample, pos_sample, cache_conv,
                                 (cache_kv_w128, cache_kv_w512, cache_kv_w2048), w, bb=Bd, ts=T, tm=Bd * T)
    return (y_p, y_s, conv_p, conv_s, bufs_p[0], bufs_p[1], bufs_p[2], bufs_s[0], bufs_s[1], bufs_s[2])
```

```python
import functools

import jax
import jax.numpy as jnp
from jax import lax
from jax.experimental import pallas as pl
from jax.experimental.pallas import tpu as pltpu

D_MODEL = 1024
DEPTH = 4
PAST_LEN = 8192
N_A = DEPTH // 2
N_B = DEPTH - N_A
CONV_WIDTH = 31
CONV_HALO = 32
SUBLANES = 8
LANES = 128
N_HEADS = 16
HEAD_DIM = D_MODEL // N_HEADS
WINDOWS = (128, 512, 2048)
DILATIONS = (1, 4, 16)
N_GROUPS = len(WINDOWS)
BLK = 128
D_FF = (8 * D_MODEL + 3 * 256 - 1) // (3 * 256) * 256
ROPE_THETA = 10000.0
EPS = 1e-6
SCALE = HEAD_DIM ** -0.5
MXU_N = 256
PROJ_SLAB = 2 * MXU_N
VMEM_LIMIT = 56 * 1024 * 1024

F32 = jnp.float32
BF16 = jnp.bfloat16


def _params(*sem):
    return pltpu.CompilerParams(dimension_semantics=sem, vmem_limit_bytes=VMEM_LIMIT)


def _rms(x, g):
    return x * lax.rsqrt(jnp.mean(x * x, axis=-1, keepdims=True) + EPS) * g


def _sigmoid(x):
    return 1.0 / (1.0 + jnp.exp(-x))


def _conv_kernel(x_ref, prev_ref, ng_ref, win_ref, bin_ref, wdw_ref, bdw_ref, lng_ref, lnb_ref,
                 wout_ref, bout_ref, y_ref, st_ref, ext_ref, *, ts):
    s = pl.program_id(1)
    bb = x_ref.shape[0]

    @pl.when(s == 0)
    def _():
        ext_ref[:, 0:CONV_HALO, :] = prev_ref[...]

    @pl.when(s > 0)
    def _():
        ext_ref[:, 0:CONV_HALO, :] = ext_ref[:, ts:ts + CONV_HALO, :]

    x = x_ref[...].reshape(bb * ts, D_MODEL)
    u = _rms(x, ng_ref[...]).astype(BF16)
    a = jnp.dot(u, win_ref[...], preferred_element_type=F32) + bin_ref[...]
    glu = a[:, :D_MODEL] * _sigmoid(a[:, D_MODEL:])
    ext_ref[:, CONV_HALO:CONV_HALO + ts, :] = glu.reshape(bb, ts, D_MODEL)

    off = CONV_HALO - (CONV_WIDTH - 1)
    c = None
    for b in range(SUBLANES):
        rows = ts if b == 0 else ts + SUBLANES
        yb = None
        for a_ in range((off + CONV_WIDTH - 1) // SUBLANES + 1):
            j = SUBLANES * a_ + b - off
            if 0 <= j < CONV_WIDTH:
                term = ext_ref[:, SUBLANES * a_:SUBLANES * a_ + rows, :] * wdw_ref[j:j + 1, :]
                yb = term if yb is None else yb + term
        yb = yb[:, b:b + ts, :]
        c = yb if c is None else c + yb
    c = c.reshape(bb * ts, D_MODEL) + bdw_ref[...]
    mu = jnp.mean(c, axis=-1, keepdims=True)
    cc = c - mu
    yn = cc * lax.rsqrt(jnp.mean(cc * cc, axis=-1, keepdims=True) + EPS) * lng_ref[...] + lnb_ref[...]
    h = (yn * _sigmoid(yn)).astype(BF16)
    out = x + jnp.dot(h, wout_ref[...], preferred_element_type=F32) + bout_ref[...]
    y_ref[...] = out.reshape(bb, ts, D_MODEL)
    st_ref[...] = ext_ref[:, ts:ts + CONV_HALO, :]


def _conv_layer(x, prev32, ng, win, bin_, wdw, bdw, lng, lnb, wout, bout, *, bb, ts):
    B, S, _ = x.shape
    full = lambda shape: pl.BlockSpec(shape, lambda b, s: (0,) * len(shape))
    return pl.pallas_call(
        functools.partial(_conv_kernel, ts=ts),
        grid=(B // bb, S // ts),
        in_specs=[
            pl.BlockSpec((bb, ts, D_MODEL), lambda b, s: (b, s, 0)),
            pl.BlockSpec((bb, CONV_HALO, D_MODEL), lambda b, s: (b, 0, 0)),
            full((1, D_MODEL)), full((D_MODEL, 2 * D_MODEL)), full((1, 2 * D_MODEL)),
            full((CONV_WIDTH, D_MODEL)), full((1, D_MODEL)), full((1, D_MODEL)), full((1, D_MODEL)),
            full((D_MODEL, D_MODEL)), full((1, D_MODEL)),
        ],
        out_specs=[
            pl.BlockSpec((bb, ts, D_MODEL), lambda b, s: (b, s, 0)),
            pl.BlockSpec((bb, CONV_HALO, D_MODEL), lambda b, s: (b, 0, 0)),
        ],
        out_shape=[jax.ShapeDtypeStruct((B, S, D_MODEL), F32),
                   jax.ShapeDtypeStruct((B, CONV_HALO, D_MODEL), F32)],
        scratch_shapes=[pltpu.VMEM((bb, CONV_HALO + ts, D_MODEL), F32)],
        compiler_params=_params("parallel", "arbitrary"),
        name="conv_module",
    )(x, prev32, ng, win, bin_, wdw, bdw, lng, lnb, wout, bout)


FF_SLAB = 256


def _ffn_kernel(x_ref, g_ref, wgu_ref, wd_ref, y_ref, h_ref):
    x = x_ref[...]
    xn = _rms(x, g_ref[...]).astype(BF16)
    for k in range(D_FF // FF_SLAB):
        gate = jnp.dot(xn, wgu_ref[:, k * FF_SLAB:(k + 1) * FF_SLAB], preferred_element_type=F32)
        up = jnp.dot(xn, wgu_ref[:, D_FF + k * FF_SLAB:D_FF + (k + 1) * FF_SLAB], preferred_element_type=F32)
        h_ref[:, k * FF_SLAB:(k + 1) * FF_SLAB] = (gate * _sigmoid(gate) * up).astype(BF16)
    y_ref[...] = x + jnp.dot(h_ref[...], wd_ref[...], preferred_element_type=F32)


def _ffn_layer(x, g, wgu, wd, *, tm):
    M = x.shape[0]
    resident = lambda shape: pl.BlockSpec(shape, lambda i: (0, 0), pipeline_mode=pl.Buffered(1))
    return pl.pallas_call(
        _ffn_kernel,
        grid=(M // tm,),
        in_specs=[
            pl.BlockSpec((tm, D_MODEL), lambda i: (i, 0)),
            pl.BlockSpec((1, D_MODEL), lambda i: (0, 0)),
            resident((D_MODEL, 2 * D_FF)),
            resident((D_FF, D_MODEL)),
        ],
        out_specs=pl.BlockSpec((tm, D_MODEL), lambda i: (i, 0)),
        out_shape=jax.ShapeDtypeStruct((M, D_MODEL), F32),
        scratch_shapes=[pltpu.VMEM((tm, D_FF), BF16)],
        compiler_params=_params("parallel"),
        name="swiglu_ffn",
    )(x, g, wgu, wd)


def _normed_slab(y, gain, cos, sin, e):
    ssq = jnp.dot((y * y).astype(BF16), e, preferred_element_type=F32)
    yn = y * lax.rsqrt(ssq * (1.0 / HEAD_DIM) + EPS) * gain
    lane = lax.broadcasted_iota(jnp.int32, (y.shape[0], LANES), 1)
    lo_half = (lane % HEAD_DIM) < (HEAD_DIM // 2)
    outs = []
    for k in range(MXU_N // LANES):
        z = yn[:, k * LANES:(k + 1) * LANES]
        partner = jnp.where(lo_half, pltpu.roll(z, LANES - HEAD_DIM // 2, 1), pltpu.roll(z, HEAD_DIM // 2, 1))
        outs.append(z * cos + partner * sin)
    return jnp.concatenate(outs, axis=-1)


def _in_order(pt, v):
    if v.dtype == BF16:
        return jnp.dot(pt, v, preferred_element_type=F32)
    hi = v.astype(BF16)
    rest = v - hi.astype(F32)
    mid = rest.astype(BF16)
    lo = (rest - mid.astype(F32)).astype(BF16)
    return (jnp.dot(pt, hi, preferred_element_type=F32) + jnp.dot(pt, mid, preferred_element_type=F32)
            + jnp.dot(pt, lo, preferred_element_type=F32))


def _proj_kernel(*refs, tm, n_chunks, cps, plain_every, tails, n_s, rs):
    x_ref, g_ref, w_ref, gain_ref, cos_ref, sin_ref, e_ref = refs[:7]
    perm_of = {g: k for k, g in enumerate(g for g, r in enumerate(rs) if r > 1)}
    n_perm = len(perm_of)
    perm_refs = refs[7:7 + n_perm]
    n_un = n_perm if tails else 0
    unperm_refs = refs[7 + n_perm:7 + n_perm + n_un]
    first_out = 7 + n_perm + n_un
    ob_refs = refs[first_out:first_out + n_chunks]
    tail_refs = refs[first_out + n_chunks:first_out + n_chunks + len(tails)]
    xn_ref = refs[-1]
    i = pl.program_id(0)
    c = pl.program_id(1)
    s = i % n_s

    @pl.when(c == 0)
    def _():
        xn = _rms(x_ref[...], g_ref[...]).astype(BF16)
        for g, r in enumerate(rs):
            if r == 1:
                xn_ref[g] = xn
            else:
                xn_ref[g] = jnp.dot(perm_refs[perm_of[g]][...], xn, preferred_element_type=F32).astype(BF16)

    def chunk(g, w0, ob_ref, normed, tail_ref, tail_len, col0):
        xn = xn_ref[g]
        r = rs[g]
        rows = tm // r
        for k in range(D_MODEL // PROJ_SLAB):
            y = jnp.dot(xn, w_ref[:, w0 + k * PROJ_SLAB:w0 + (k + 1) * PROJ_SLAB], preferred_element_type=F32)
            for h in range(PROJ_SLAB // MXU_N):
                sl = slice(k * PROJ_SLAB + h * MXU_N, k * PROJ_SLAB + (h + 1) * MXU_N)
                z = y[:, h * MXU_N:(h + 1) * MXU_N]
                if normed:
                    z = _normed_slab(z, gain_ref[g, :, sl], cos_ref[g], sin_ref[g], e_ref[...])
                zb = z.astype(BF16)
                for cc in range(r):
                    ob_ref[0, cc, 0, :, sl] = zb[cc * rows:(cc + 1) * rows]
                if tail_ref is not None:
                    tsl = slice(col0 + sl.start, col0 + sl.stop)
                    if tail_len >= tm:
                        @pl.when(s >= n_s - tail_len // tm)
                        def _():
                            tail_ref[0, :, tsl] = z if r == 1 else _in_order(unperm_refs[perm_of[g]][...], z)
                    else:
                        assert r == 1
                        @pl.when(s == n_s - 1)
                        def _():
                            tail_ref[0, :, tsl] = z[tm - tail_len:, :]

    def step(st):
        for u in range(cps):
            cs = st * cps + u
            g, part = divmod(cs, max(plain_every, 1))
            tail = (tail_refs[g], tails[g]) if tails else (None, 0)
            chunk(g, u * D_MODEL, ob_refs[cs], part == 0, *tail, part * D_MODEL)

    if n_chunks == cps:
        step(0)
    else:
        for st in range(n_chunks // cps):
            pl.when(c == st)(functools.partial(step, st))


def _residue_major(n, r):
    pos = jnp.arange(n, dtype=jnp.int32)
    return (pos % (n // r)) * r + pos // (n // r)


def _proj(x, g, w, gain, cos, sin, e, *, tm, plain_every, cps, rs, seq=None, tails=()):
    M = x.shape[0]
    n_chunks = w.shape[1] // D_MODEL
    gdiv = max(plain_every, 1)
    seq = seq or M
    n_s = seq // tm
    n_pos_tiles = cos.shape[0] // tm
    orders = [_residue_major(tm, r) for r in rs]
    n_of = jnp.arange(tm, dtype=jnp.int32)
    perms = [(o[:, None] == n_of[None, :]).astype(BF16) for o, r in zip(orders, rs) if r > 1]
    if tails:
        perms += [(n_of[:, None] == o[None, :]).astype(BF16) for o, r in zip(orders, rs) if r > 1]
    tile_order = lambda tab, o: tab.reshape(n_pos_tiles, tm, LANES)[:, o].reshape(n_pos_tiles * tm, LANES)
    cos = jnp.stack([tile_order(cos, o) for o in orders])
    sin = jnp.stack([tile_order(sin, o) for o in orders])
    out_shape, out_specs = [], []
    for cs in range(n_chunks):
        r = rs[cs // gdiv]
        out_shape.append(jax.ShapeDtypeStruct((M // seq, r, n_s, tm // r, D_MODEL), BF16))
        out_specs.append(pl.BlockSpec((1, r, 1, tm // r, D_MODEL), lambda i, c: (i // n_s, 0, i % n_s, 0, 0)))
    for lg in tails:
        out_shape.append(jax.ShapeDtypeStruct((M // seq, lg, 2 * D_MODEL), F32))
        if lg >= tm:
            first = n_s - lg // tm
            out_specs.append(pl.BlockSpec((1, tm, 2 * D_MODEL),
                                          lambda i, c, first=first: (i // n_s, jnp.maximum(i % n_s - first, 0), 0)))
        else:
            out_specs.append(pl.BlockSpec((1, lg, 2 * D_MODEL), lambda i, c: (i // n_s, 0, 0)))
    n_groups = len(rs)
    table = pl.BlockSpec((n_groups, tm, LANES), lambda i, c: (0, i % n_pos_tiles, 0))
    outs = pl.pallas_call(
        functools.partial(_proj_kernel, tm=tm, n_chunks=n_chunks, cps=cps, plain_every=plain_every,
                          tails=tuple(tails), n_s=n_s, rs=tuple(rs)),
        grid=(M // tm, n_chunks // cps),
        in_specs=[
            pl.BlockSpec((tm, D_MODEL), lambda i, c: (i, 0)),
            pl.BlockSpec((1, D_MODEL), lambda i, c: (0, 0)),
            pl.BlockSpec((D_MODEL, cps * D_MODEL), lambda i, c: (0, c)),
            pl.BlockSpec((n_groups, 1, D_MODEL), lambda i, c: (0, 0, 0)),
            table, table,
            pl.BlockSpec((MXU_N, MXU_N), lambda i, c: (0, 0)),
        ] + [pl.BlockSpec((tm, tm), lambda i, c: (0, 0))] * len(perms),
        out_specs=out_specs,
        out_shape=out_shape,
        scratch_shapes=[pltpu.VMEM((len(rs), tm, D_MODEL), BF16)],
        compiler_params=_params("arbitrary", "arbitrary"),
        name="norm_proj_rotary",
    )(x, g, w, gain, cos, sin, e, *perms)
    chunks = [o.reshape(M // seq * rs[cs // gdiv], seq // rs[cs // gdiv], D_MODEL) for cs, o in enumerate(outs[:n_chunks])]
    return chunks + list(outs[n_chunks:])


def _rope_tables(pos):
    half = HEAD_DIM // 2
    inv = ROPE_THETA ** (-jnp.arange(half, dtype=F32) / half)
    ang = pos.astype(F32)[:, None] * inv[None, :]
    cos, sin = jnp.cos(ang), jnp.sin(ang)
    reps = LANES // HEAD_DIM
    return jnp.tile(cos, (1, 2 * reps)), jnp.tile(jnp.concatenate([-sin, sin], axis=1), (1, reps))


def _qk(q, k):
    return lax.dot_general(q, k, (((1,), (1,)), ((), ())), preferred_element_type=F32)


def _head_masks(rows):
    first = lax.broadcasted_iota(jnp.int32, (rows, LANES), 1) < HEAD_DIM
    return first


def _band_attn_kernel(q_ref, k_ref, v_ref, o_ref, lse_ref, kf_ref, vf_ref, s_ref, p_ref, *, tq):
    n = pl.program_id(2)
    rows = o_ref.shape[3]

    def store(ref, i, sl, val):
        if rows >= BLK:
            per = rows // BLK
            ref[0, i // per, 0, pl.ds(pl.multiple_of((i % per) * BLK, BLK), BLK), sl] = val
        else:
            for u in range(BLK // rows):
                ref[0, i * (BLK // rows) + u, 0, :, sl] = val[u * rows:(u + 1) * rows]

    @pl.when(n == 0)
    def _():
        kf_ref[0:BLK, :] = jnp.zeros((BLK, D_MODEL), BF16)
        vf_ref[0:BLK, :] = jnp.zeros((BLK, D_MODEL), BF16)

    @pl.when(n > 0)
    def _():
        kf_ref[0:BLK, :] = kf_ref[tq:tq + BLK, :]
        vf_ref[0:BLK, :] = vf_ref[tq:tq + BLK, :]

    kf_ref[BLK:BLK + tq, :] = k_ref[0]
    vf_ref[BLK:BLK + tq, :] = v_ref[0]

    qi = lax.broadcasted_iota(jnp.int32, (BLK, 2 * BLK), 0)
    ki = lax.broadcasted_iota(jnp.int32, (BLK, 2 * BLK), 1)
    dist = BLK + qi - ki
    band = (dist >= 0) & (dist <= BLK)
    first = _head_masks(BLK)
    lane = lax.broadcasted_iota(jnp.int32, (BLK, LANES), 1)
    zero = jnp.zeros((BLK, LANES), BF16)

    def block(i, carry):
        r0 = pl.multiple_of(i * BLK, BLK)
        mask = band & ((ki >= BLK) | (n + i > 0))
        for p in range(N_HEADS // 2):
            sl = slice(p * LANES, (p + 1) * LANES)
            qp = q_ref[0, pl.ds(r0, BLK), sl]
            kp = kf_ref[pl.ds(r0, 2 * BLK), sl]
            s_ref[2 * p] = jnp.where(mask, _qk(jnp.where(first, qp, zero), kp), -jnp.inf)
            s_ref[2 * p + 1] = jnp.where(mask, _qk(jnp.where(first, zero, qp), kp), -jnp.inf)
        s = s_ref[...]
        m = jnp.max(s, axis=-1, keepdims=True)
        e = jnp.exp(s - m)
        l = jnp.sum(e, axis=-1, keepdims=True)
        p_ref[...] = e.astype(BF16)
        inv = 1.0 / l
        lse = m + jnp.log(l)
        lse_tile = jnp.zeros((BLK, LANES), F32)
        for p in range(N_HEADS // 2):
            sl = slice(p * LANES, (p + 1) * LANES)
            vp = vf_ref[pl.ds(r0, 2 * BLK), sl]
            oa = jnp.dot(p_ref[2 * p], vp, preferred_element_type=F32) * inv[2 * p]
            ob = jnp.dot(p_ref[2 * p + 1], vp, preferred_element_type=F32) * inv[2 * p + 1]
            store(o_ref, i, sl, jnp.where(first, oa, ob).astype(BF16))
            lse_tile = jnp.where(lane == 2 * p, lse[2 * p], lse_tile)
            lse_tile = jnp.where(lane == 2 * p + 1, lse[2 * p + 1], lse_tile)
        store(lse_ref, i, slice(None), lse_tile)
        return carry

    lax.fori_loop(0, tq // BLK, block, 0)


def _band_attention(q, k, v, g, *, B, S, tm):
    r = DILATIONS[g]
    L = S // r
    tq = min(L, 4 * BLK)
    rows = tm // r
    nt = tq // rows
    spec = pl.BlockSpec((1, tq, D_MODEL), lambda b, c, n: (b * r + c, n, 0))
    o, lse = pl.pallas_call(
        functools.partial(_band_attn_kernel, tq=tq),
        grid=(B, r, L // tq),
        in_specs=[spec, spec, spec],
        out_specs=[
            pl.BlockSpec((1, nt, 1, rows, D_MODEL), lambda b, c, n: (b, n, c, 0, 0)),
            pl.BlockSpec((1, nt, 1, rows, LANES), lambda b, c, n: (b, n, c, 0, 0)),
        ],
        out_shape=[jax.ShapeDtypeStruct((B, S // tm, r, rows, D_MODEL), BF16),
                   jax.ShapeDtypeStruct((B, S // tm, r, rows, LANES), F32)],
        scratch_shapes=[pltpu.VMEM((BLK + tq, D_MODEL), BF16), pltpu.VMEM((BLK + tq, D_MODEL), BF16),
                        pltpu.VMEM((N_HEADS, BLK, 2 * BLK), F32), pltpu.VMEM((N_HEADS, BLK, 2 * BLK), BF16)],
        compiler_params=_params("parallel", "parallel", "arbitrary"),
        name=f"band_attention_g{g}",
    )(q, k, v)
    return o.reshape(B * S // tm, r, rows, D_MODEL), lse.reshape(B * S // tm, r, rows, LANES)


NEW_PAD = LANES


def _cache_attn_kernel(*refs, lb, r, t, hb, roll):
    if roll:
        q_ref, kn_ref, vn_ref, newt_ref, c_ref, o_ref, lse_ref, roll_ref, new_ref = refs
    else:
        q_ref, kn_ref, vn_ref, c_ref, o_ref, lse_ref, new_ref = refs
    j = pl.program_id(1)
    cols = hb * HEAD_DIM
    new_ref[...] = jnp.zeros((NEW_PAD, 2 * cols), BF16)
    new_ref[0:t, 0:cols] = kn_ref[0]
    new_ref[0:t, cols:2 * cols] = vn_ref[0]

    ti = lax.broadcasted_iota(jnp.int32, (t, NEW_PAD + lb), 0)
    rho = lax.broadcasted_iota(jnp.int32, (t, NEW_PAD + lb), 1)
    idx = jnp.where(rho < NEW_PAD, lb + rho, rho - NEW_PAD)
    d = lb + ti - idx
    mask = ((rho < t) | (rho >= NEW_PAD)) & (d >= 0) & (d <= BLK * r) & ((d & (r - 1)) == 0)
    first = _head_masks(t)
    lane = lax.broadcasted_iota(jnp.int32, (t, LANES), 1)
    zero = jnp.zeros((t, LANES), BF16)

    lse_tile = jnp.zeros((t, LANES), F32)
    for p in range(hb // 2):
        ks = slice(p * LANES, (p + 1) * LANES)
        vs = slice(cols + p * LANES, cols + (p + 1) * LANES)
        qp = q_ref[0, :, ks]
        k2 = c_ref[0, 0, 2 * p:2 * p + 2].reshape(LANES, lb).astype(BF16)
        v2 = c_ref[0, 1, 2 * p:2 * p + 2].reshape(LANES, lb).astype(BF16)
        outs = []
        for hh, qm in enumerate((jnp.where(first, qp, zero), jnp.where(first, zero, qp))):
            s = jnp.concatenate([_qk(qm, new_ref[:, ks]), jnp.dot(qm, k2, preferred_element_type=F32)], axis=1)
            s = jnp.where(mask, s, -jnp.inf)
            m = jnp.max(s, axis=-1, keepdims=True)
            e = jnp.exp(s - m)
            l = jnp.sum(e, axis=-1, keepdims=True)
            eb = e.astype(BF16)
            o = (jnp.dot(eb[:, :NEW_PAD], new_ref[:, vs], preferred_element_type=F32)
                 + _qk(eb[:, NEW_PAD:], v2)) * (1.0 / l)
            outs.append(o)
            lse_tile = jnp.where(lane == j * hb + 2 * p + hh, m + jnp.log(l), lse_tile)
        o_ref[0, :, ks] = jnp.where(first, outs[0], outs[1])
    lse_ref[0, 0] = lse_tile

    if roll:
        for kv in range(2):
            for h in range(hb):
                roll_ref[0, kv, h] = pltpu.roll(c_ref[0, kv, h], lb - t, 1)
                roll_ref[0, kv, h, :, lb - t:lb] = newt_ref[0, kv, h]


CACHE_BLOCK_BYTES = 8 * 1024 * 1024


def _cache_attention(q, kn, vn, cache_t, g, new_t=None):
    B, T, _ = q.shape
    lb = cache_t.shape[-1]
    r = DILATIONS[g]
    hb = N_HEADS
    while 2 * hb * HEAD_DIM * lb * 4 > CACHE_BLOCK_BYTES:
        hb //= 2
    nj = N_HEADS // hb
    roll = new_t is not None
    new = pl.BlockSpec((1, T, hb * HEAD_DIM), lambda b, j: (b, 0, j))
    heads = lambda w: pl.BlockSpec((1, 2, hb, HEAD_DIM, w), lambda b, j: (b, 0, j, 0, 0))
    out_specs = [new, pl.BlockSpec((1, 1, T, LANES), lambda b, j: (b, j, 0, 0))]
    out_shape = [jax.ShapeDtypeStruct((B, T, D_MODEL), F32), jax.ShapeDtypeStruct((B, nj, T, LANES), F32)]
    if roll:
        out_specs.append(heads(lb))
        out_shape.append(jax.ShapeDtypeStruct(cache_t.shape, cache_t.dtype))
    o, lse, *rolled = pl.pallas_call(
        functools.partial(_cache_attn_kernel, lb=lb, r=r, t=T, hb=hb, roll=roll),
        grid=(B, nj),
        in_specs=[new, new, new] + ([heads(T)] if roll else []) + [heads(lb)],
        out_specs=out_specs,
        out_shape=out_shape,
        scratch_shapes=[pltpu.VMEM((NEW_PAD, 2 * hb * HEAD_DIM), BF16)],
        compiler_params=_params("parallel", "parallel"),
        name=f"cache_attention_g{g}",
    )(q, kn, vn, *([new_t] if roll else []), cache_t)
    return o.reshape(B * T, D_MODEL), jnp.sum(lse, axis=1).reshape(B * T, LANES), rolled


def _merge_kernel(*refs, rs):
    n = len(rs)
    x_ref = refs[0]
    o_refs = refs[1:1 + n]
    l_refs = refs[1 + n:1 + 2 * n]
    pt_refs = refs[1 + 2 * n:-2]
    wo_ref, y_ref = refs[-2:]
    tm = x_ref.shape[0]
    os_, ls = [], []
    k = 0
    for o_ref, l_ref, r in zip(o_refs, l_refs, rs):
        o = o_ref[0].reshape(tm, D_MODEL)
        l = l_ref[0].reshape(tm, LANES)
        if r > 1:
            o, l = _in_order(pt_refs[k][...], o), _in_order(pt_refs[k][...], l)
            k += 1
        os_.append(o)
        ls.append(l)
    m = functools.reduce(jnp.maximum, ls)
    es = [jnp.exp(l - m) for l in ls]
    inv = 1.0 / functools.reduce(lambda a, b: a + b, es)
    ws = [e * inv for e in es]
    first = _head_masks(tm)
    cols = []
    for p in range(N_HEADS // 2):
        sl = slice(p * LANES, (p + 1) * LANES)
        acc = None
        for w, o in zip(ws, os_):
            wp = jnp.where(first, w[:, 2 * p:2 * p + 1], w[:, 2 * p + 1:2 * p + 2])
            term = wp * o[:, sl].astype(F32)
            acc = term if acc is None else acc + term
        cols.append(acc.astype(BF16))
    merged = jnp.concatenate(cols, axis=-1)
    y_ref[...] = x_ref[...] + jnp.dot(merged, wo_ref[...], preferred_element_type=F32)


def _merge_layer(x, outs, lses, wo, *, tm, rs):
    M = x.shape[0]
    row = lambda w: pl.BlockSpec((tm, w), lambda i: (i, 0))
    part = lambda r, w: pl.BlockSpec((1, r, tm // r, w), lambda i: (i, 0, 0, 0))
    n_of = jnp.arange(tm, dtype=jnp.int32)
    pts = [(n_of[:, None] == _residue_major(tm, r)[None, :]).astype(BF16) for r in rs if r > 1]
    return pl.pallas_call(
        functools.partial(_merge_kernel, rs=tuple(rs)),
        grid=(M // tm,),
        in_specs=([row(D_MODEL)] + [part(r, D_MODEL) for r in rs] + [part(r, LANES) for r in rs]
                  + [pl.BlockSpec((tm, tm), lambda i: (0, 0))] * len(pts)
                  + [pl.BlockSpec((D_MODEL, D_MODEL), lambda i: (0, 0))]),
        out_specs=row(D_MODEL),
        out_shape=jax.ShapeDtypeStruct((M, D_MODEL), F32),
        compiler_params=_params("parallel"),
        name="merge_groups_wo",
    )(x, *outs, *lses, *pts, wo)


def _seq_minor(a):
    return jnp.transpose(a, (0, 2, 3, 4, 1))


def _seq_major(a):
    return jnp.transpose(a, (0, 4, 1, 2, 3))


def _trunk(x, pos, conv_prev, kv_caches, w, *, bb, ts, tm):
    B, S, _ = x.shape
    M = B * S
    cos, sin = _rope_tables(pos)
    if S < tm:
        cos, sin = jnp.tile(cos, (tm // S, 1)), jnp.tile(sin, (tm // S, 1))
    prev32 = jnp.pad(conv_prev, ((0, 0), (0, 0), (CONV_HALO - (CONV_WIDTH - 1), 0), (0, 0)))

    rs = DILATIONS if kv_caches is None else (1,) * N_GROUPS
    seq = S if kv_caches is None else M
    conv_states = []
    new_bufs = []
    kv = None
    for layer in range(DEPTH):
        if layer < N_A:
            x, st = _conv_layer(x, prev32[layer], w['a_norm'][layer], w['a_w_in'][layer], w['a_b_in'][layer],
                                w['a_w_dw'][layer], w['a_b_dw'][layer], w['a_ln_g'][layer], w['a_ln_b'][layer],
                                w['a_w_out'][layer], w['a_b_out'][layer], bb=bb, ts=ts)
            conv_states.append(st[:, CONV_HALO - (CONV_WIDTH - 1):])
            xf = x.reshape(M, D_MODEL)
        else:
            j = layer - N_A
            if layer == N_A:
                tails = tuple(min(wd, S) for wd in WINDOWS) if kv_caches is None else (M,) * N_GROUPS
                *kv, t0, t1, t2 = _proj(xf, w['kv_norm'], w['w_kv'], w['k_gain'], cos, sin, w['head_ones'],
                                        tm=tm, plain_every=2, cps=2, rs=rs, seq=seq, tails=tails)
                tails_f32 = [t.reshape(B, -1, 2, N_HEADS, HEAD_DIM) for t in (t0, t1, t2)]
                if kv_caches is None:
                    new_bufs = tails_f32
            q = _proj(xf, w['b_norm'][j], w['w_q'][j], w['q_gain'][j], cos, sin, w['head_ones'],
                      tm=tm, plain_every=0, cps=N_GROUPS, rs=rs, seq=seq)
            outs, lses = [], []
            for g in range(N_GROUPS):
                if kv_caches is None:
                    o, l = _band_attention(q[g], kv[2 * g], kv[2 * g + 1], g, B=B, S=S, tm=tm)
                else:
                    new = lambda a: a.reshape(B, S, D_MODEL)
                    new_t = _seq_minor(tails_f32[g]) if layer == N_A else None
                    o, l, rolled = _cache_attention(new(q[g]), new(kv[2 * g]), new(kv[2 * g + 1]),
                                                    _seq_minor(kv_caches[g]), g, new_t)
                    if rolled:
                        new_bufs.append(_seq_major(rolled[0]))
                    o, l = o.reshape(M // tm, 1, tm, D_MODEL), l.reshape(M // tm, 1, tm, LANES)
                outs.append(o)
                lses.append(l)
            xf = _merge_layer(xf, outs, lses, w['w_o'][j], tm=tm, rs=rs)
        xf = _ffn_layer(xf, w['ffn_norm'][layer], w['w_gate_up'][layer], w['w_down'][layer], tm=tm)
        x = xf.reshape(B, S, D_MODEL)
    return x, jnp.stack(conv_states, axis=0), new_bufs


def kernel(x_prompt, x_sample, cache_conv, cache_kv_w128, cache_kv_w512, cache_kv_w2048,
           a_norm, a_w_in, a_b_in, a_w_dw, a_b_dw, a_ln_g, a_ln_b, a_w_out, a_b_out,
           kv_norm, w_kv, k_norm, b_norm, w_q, q_norm, w_o, ffn_norm, w_gate_up, w_down):
    B, S, _ = x_prompt.shape
    Bd, T, _ = x_sample.shape
    row = lambda a: a.reshape(a.shape[:-1] + (1, a.shape[-1]))
    head_of = jnp.arange(MXU_N, dtype=jnp.int32) // HEAD_DIM
    w = dict(
        a_norm=row(a_norm), a_w_in=a_w_in.astype(BF16), a_b_in=row(a_b_in), a_w_dw=a_w_dw, a_b_dw=row(a_b_dw),
        a_ln_g=row(a_ln_g), a_ln_b=row(a_ln_b), a_w_out=a_w_out.astype(BF16), a_b_out=row(a_b_out),
        kv_norm=row(kv_norm), w_kv=w_kv.astype(BF16), b_norm=row(b_norm), w_q=w_q.astype(BF16),
        w_o=w_o.astype(BF16), ffn_norm=row(ffn_norm), w_gate_up=w_gate_up.astype(BF16), w_down=w_down.astype(BF16),
        k_gain=row(jnp.tile(k_norm, (1, N_HEADS))),
        q_gain=row(jnp.tile(q_norm, (1, 1, N_HEADS)) * SCALE),
        head_ones=(head_of[:, None] == head_of[None, :]).astype(BF16),
    )
    pos_prompt = jnp.arange(S, dtype=jnp.int32)
    pos_sample = PAST_LEN + jnp.arange(T, dtype=jnp.int32)
    conv_zero = jnp.zeros((N_A, B, CONV_WIDTH - 1, D_MODEL), x_prompt.dtype)
    y_p, conv_p, bufs_p = _trunk(x_prompt, pos_prompt, conv_zero, None, w, bb=1, ts=512, tm=512)
    y_s, conv_s, bufs_s = _trunk(x_sample, pos_sample, cache_conv,
                                 (cache_kv_w128, cache_kv_w512, cache_kv_w2048), w, bb=Bd, ts=T, tm=Bd * T)
    return (y_p, y_s, conv_p, conv_s, bufs_p[0], bufs_p[1], bufs_p[2], bufs_s[0], bufs_s[1], bufs_s[2])
```

```python
import functools

import jax
import jax.numpy as jnp
from jax import lax
from jax.experimental import pallas as pl
from jax.experimental.pallas import tpu as pltpu

D_MODEL = 1024
DEPTH = 4
PAST_LEN = 8192
N_A = DEPTH // 2
N_B = DEPTH - N_A
CONV_WIDTH = 31
CONV_HALO = 32
SUBLANES = 8
LANES = 128
N_HEADS = 16
HEAD_DIM = D_MODEL // N_HEADS
WINDOWS = (128, 512, 2048)
DILATIONS = (1, 4, 16)
N_GROUPS = len(WINDOWS)
BLK = 128
D_FF = (8 * D_MODEL + 3 * 256 - 1) // (3 * 256) * 256
ROPE_THETA = 10000.0
EPS = 1e-6
SCALE = HEAD_DIM ** -0.5
MXU_N = 256
PROJ_SLAB = 2 * MXU_N
VMEM_LIMIT = 56 * 1024 * 1024

F32 = jnp.float32
BF16 = jnp.bfloat16


def _params(*sem):
    return pltpu.CompilerParams(dimension_semantics=sem, vmem_limit_bytes=VMEM_LIMIT)


def _rms(x, g):
    return x * lax.rsqrt(jnp.mean(x * x, axis=-1, keepdims=True) + EPS) * g


def _sigmoid(x):
    return 1.0 / (1.0 + jnp.exp(-x))


def _conv_kernel(x_ref, prev_ref, ng_ref, win_ref, bin_ref, wdw_ref, bdw_ref, lng_ref, lnb_ref,
                 wout_ref, bout_ref, y_ref, st_ref, ext_ref, *, ts):
    s = pl.program_id(1)
    bb = x_ref.shape[0]

    @pl.when(s == 0)
    def _():
        ext_ref[:, 0:CONV_HALO, :] = prev_ref[...]

    @pl.when(s > 0)
    def _():
        ext_ref[:, 0:CONV_HALO, :] = ext_ref[:, ts:ts + CONV_HALO, :]

    x = x_ref[...].reshape(bb * ts, D_MODEL)
    u = _rms(x, ng_ref[...]).astype(BF16)
    a = jnp.dot(u, win_ref[...], preferred_element_type=F32) + bin_ref[...]
    glu = a[:, :D_MODEL] * _sigmoid(a[:, D_MODEL:])
    ext_ref[:, CONV_HALO:CONV_HALO + ts, :] = glu.reshape(bb, ts, D_MODEL)

    off = CONV_HALO - (CONV_WIDTH - 1)
    c = None
    for b in range(SUBLANES):
        rows = ts if b == 0 else ts + SUBLANES
        yb = None
        for a_ in range((off + CONV_WIDTH - 1) // SUBLANES + 1):
            j = SUBLANES * a_ + b - off
            if 0 <= j < CONV_WIDTH:
                term = ext_ref[:, SUBLANES * a_:SUBLANES * a_ + rows, :] * wdw_ref[j:j + 1, :]
                yb = term if yb is None else yb + term
        yb = yb[:, b:b + ts, :]
        c = yb if c is None else c + yb
    c = c.reshape(bb * ts, D_MODEL) + bdw_ref[...]
    mu = jnp.mean(c, axis=-1, keepdims=True)
    cc = c - mu
    yn = cc * lax.rsqrt(jnp.mean(cc * cc, axis=-1, keepdims=True) + EPS) * lng_ref[...] + lnb_ref[...]
    h = (yn * _sigmoid(yn)).astype(BF16)
    out = x + jnp.dot(h, wout_ref[...], preferred_element_type=F32) + bout_ref[...]
    y_ref[...] = out.reshape(bb, ts, D_MODEL)
    st_ref[...] = ext_ref[:, ts:ts + CONV_HALO, :]


def _conv_layer(x, prev32, ng, win, bin_, wdw, bdw, lng, lnb, wout, bout, *, bb, ts):
    B, S, _ = x.shape
    full = lambda shape: pl.BlockSpec(shape, lambda b, s: (0,) * len(shape))
    return pl.pallas_call(
        functools.partial(_conv_kernel, ts=ts),
        grid=(B // bb, S // ts),
        in_specs=[
            pl.BlockSpec((bb, ts, D_MODEL), lambda b, s: (b, s, 0)),
            pl.BlockSpec((bb, CONV_HALO, D_MODEL), lambda b, s: (b, 0, 0)),
            full((1, D_MODEL)), full((D_MODEL, 2 * D_MODEL)), full((1, 2 * D_MODEL)),
            full((CONV_WIDTH, D_MODEL)), full((1, D_MODEL)), full((1, D_MODEL)), full((1, D_MODEL)),
            full((D_MODEL, D_MODEL)), full((1, D_MODEL)),
        ],
        out_specs=[
            pl.BlockSpec((bb, ts, D_MODEL), lambda b, s: (b, s, 0)),
            pl.BlockSpec((bb, CONV_HALO, D_MODEL), lambda b, s: (b, 0, 0)),
        ],
        out_shape=[jax.ShapeDtypeStruct((B, S, D_MODEL), F32),
                   jax.ShapeDtypeStruct((B, CONV_HALO, D_MODEL), F32)],
        scratch_shapes=[pltpu.VMEM((bb, CONV_HALO + ts, D_MODEL), F32)],
        compiler_params=_params("parallel", "arbitrary"),
        name="conv_module",
    )(x, prev32, ng, win, bin_, wdw, bdw, lng, lnb, wout, bout)


FF_SLAB = 256


def _ffn_kernel(x_ref, g_ref, wgu_ref, wd_ref, y_ref, h_ref):
    x = x_ref[...]
    xn = _rms(x, g_ref[...]).astype(BF16)
    for k in range(D_FF // FF_SLAB):
        gate = jnp.dot(xn, wgu_ref[:, k * FF_SLAB:(k + 1) * FF_SLAB], preferred_element_type=F32)
        up = jnp.dot(xn, wgu_ref[:, D_FF + k * FF_SLAB:D_FF + (k + 1) * FF_SLAB], preferred_element_type=F32)
        h_ref[:, k * FF_SLAB:(k + 1) * FF_SLAB] = (gate * _sigmoid(gate) * up).astype(BF16)
    y_ref[...] = x + jnp.dot(h_ref[...], wd_ref[...], preferred_element_type=F32)


def _ffn_layer(x, g, wgu, wd, *, tm):
    M = x.shape[0]
    resident = lambda shape: pl.BlockSpec(shape, lambda i: (0, 0), pipeline_mode=pl.Buffered(1))
    return pl.pallas_call(
        _ffn_kernel,
        grid=(M // tm,),
        in_specs=[
            pl.BlockSpec((tm, D_MODEL), lambda i: (i, 0)),
            pl.BlockSpec((1, D_MODEL), lambda i: (0, 0)),
            resident((D_MODEL, 2 * D_FF)),
            resident((D_FF, D_MODEL)),
        ],
        out_specs=pl.BlockSpec((tm, D_MODEL), lambda i: (i, 0)),
        out_shape=jax.ShapeDtypeStruct((M, D_MODEL), F32),
        scratch_shapes=[pltpu.VMEM((tm, D_FF), BF16)],
        compiler_params=_params("parallel"),
        name="swiglu_ffn",
    )(x, g, wgu, wd)


def _normed_slab(y, gain, cos, sin, e):
    ssq = jnp.dot((y * y).astype(BF16), e, preferred_element_type=F32)
    yn = y * lax.rsqrt(ssq * (1.0 / HEAD_DIM) + EPS) * gain
    lane = lax.broadcasted_iota(jnp.int32, (y.shape[0], LANES), 1)
    lo_half = (lane % HEAD_DIM) < (HEAD_DIM // 2)
    outs = []
    for k in range(MXU_N // LANES):
        z = yn[:, k * LANES:(k + 1) * LANES]
        partner = jnp.where(lo_half, pltpu.roll(z, LANES - HEAD_DIM // 2, 1), pltpu.roll(z, HEAD_DIM // 2, 1))
        outs.append(z * cos + partner * sin)
    return jnp.concatenate(outs, axis=-1)


def _in_order(pt, v):
    if v.dtype == BF16:
        return jnp.dot(pt, v, preferred_element_type=F32)
    hi = v.astype(BF16)
    rest = v - hi.astype(F32)
    mid = rest.astype(BF16)
    lo = (rest - mid.astype(F32)).astype(BF16)
    return (jnp.dot(pt, hi, preferred_element_type=F32) + jnp.dot(pt, mid, preferred_element_type=F32)
            + jnp.dot(pt, lo, preferred_element_type=F32))


def _proj_kernel(*refs, tm, n_chunks, cps, plain_every, tails, n_s, rs):
    x_ref, g_ref, w_ref, gain_ref, cos_ref, sin_ref, e_ref = refs[:7]
    perm_of = {g: k for k, g in enumerate(g for g, r in enumerate(rs) if r > 1)}
    n_perm = len(perm_of)
    perm_refs = refs[7:7 + n_perm]
    out_refs = refs[7 + n_perm:-1]
    ob_refs, tail_refs = ((), out_refs) if tails else (out_refs, ())
    xn_ref = refs[-1]
    i = pl.program_id(0)
    c = pl.program_id(1)
    s = i % n_s

    @pl.when(c == 0)
    def _():
        xn = _rms(x_ref[...], g_ref[...]).astype(BF16)
        for g, r in enumerate(rs):
            if r == 1:
                xn_ref[g] = xn
            else:
                xn_ref[g] = jnp.dot(perm_refs[perm_of[g]][...], xn, preferred_element_type=F32).astype(BF16)

    def chunk(g, w0, normed, store):
        xn = xn_ref[g]
        for k in range(D_MODEL // PROJ_SLAB):
            y = jnp.dot(xn, w_ref[:, w0 + k * PROJ_SLAB:w0 + (k + 1) * PROJ_SLAB], preferred_element_type=F32)
            for h in range(PROJ_SLAB // MXU_N):
                sl = slice(k * PROJ_SLAB + h * MXU_N, k * PROJ_SLAB + (h + 1) * MXU_N)
                z = y[:, h * MXU_N:(h + 1) * MXU_N]
                if normed:
                    z = _normed_slab(z, gain_ref[g, :, sl], cos_ref[g], sin_ref[g], e_ref[...])
                store(sl, z)

    def step(st):
        for u in range(cps):
            cs = st * cps + u
            g, part = divmod(cs, max(plain_every, 1))
            if not tails:
                def store(sl, z, ob_ref=ob_refs[cs], r=rs[g]):
                    zb = z.astype(BF16)
                    rows = tm // r
                    for cc in range(r):
                        ob_ref[0, cc, 0, :, sl] = zb[cc * rows:(cc + 1) * rows]
                chunk(g, u * D_MODEL, part == 0, store)
            else:
                lg = tails[g]
                keep = min(lg, tm)
                def store(sl, z, tail_ref=tail_refs[g], col0=part * D_MODEL, keep=keep):
                    tail_ref[0, :, col0 + sl.start:col0 + sl.stop] = z[tm - keep:, :]
                pl.when(s >= n_s - max(lg // tm, 1))(functools.partial(chunk, g, u * D_MODEL, part == 0, store))

    if n_chunks == cps:
        step(0)
    else:
        for st in range(n_chunks // cps):
            pl.when(c == st)(functools.partial(step, st))


def _residue_major(n, r):
    pos = jnp.arange(n, dtype=jnp.int32)
    return (pos % (n // r)) * r + pos // (n // r)


def _proj(x, g, w, gain, cos, sin, e, *, tm, plain_every, cps, rs, seq=None, tails=()):
    M = x.shape[0]
    n_chunks = w.shape[1] // D_MODEL
    gdiv = max(plain_every, 1)
    seq = seq or M
    n_groups = len(rs)
    n_seq_tiles = seq // tm
    n_pos_tiles = cos.shape[0] // tm
    n_s = max(-(-max(tails) // tm), 1) if tails else n_seq_tiles
    skip = n_seq_tiles - n_s
    orders = [_residue_major(tm, r) for r in rs]
    n_of = jnp.arange(tm, dtype=jnp.int32)
    perms = [(o[:, None] == n_of[None, :]).astype(BF16) for o, r in zip(orders, rs) if r > 1]
    tile_order = lambda tab, o: tab.reshape(n_pos_tiles, tm, LANES)[:, o].reshape(n_pos_tiles * tm, LANES)
    cos = jnp.stack([tile_order(cos, o) for o in orders])
    sin = jnp.stack([tile_order(sin, o) for o in orders])
    out_shape, out_specs = [], []
    if not tails:
        for cs in range(n_chunks):
            r = rs[cs // gdiv]
            out_shape.append(jax.ShapeDtypeStruct((M // seq, r, n_s, tm // r, D_MODEL), BF16))
            out_specs.append(pl.BlockSpec((1, r, 1, tm // r, D_MODEL), lambda i, c: (i // n_s, 0, i % n_s, 0, 0)))
    for lg in tails:
        assert all(r == 1 for r in rs)
        out_shape.append(jax.ShapeDtypeStruct((M // seq, lg, 2 * D_MODEL), F32))
        first = n_s - max(lg // tm, 1)
        out_specs.append(pl.BlockSpec((1, min(lg, tm), 2 * D_MODEL),
                                      lambda i, c, first=first: (i // n_s, jnp.maximum(i % n_s - first, 0), 0)))
    table = pl.BlockSpec((n_groups, tm, LANES), lambda i, c: (0, (skip + i % n_s) % n_pos_tiles, 0))
    outs = pl.pallas_call(
        functools.partial(_proj_kernel, tm=tm, n_chunks=n_chunks, cps=cps, plain_every=plain_every,
                          tails=tuple(tails), n_s=n_s, rs=tuple(rs)),
        grid=(M // seq * n_s, n_chunks // cps),
        in_specs=[
            pl.BlockSpec((tm, D_MODEL), lambda i, c: (i // n_s * n_seq_tiles + skip + i % n_s, 0)),
            pl.BlockSpec((1, D_MODEL), lambda i, c: (0, 0)),
            pl.BlockSpec((D_MODEL, cps * D_MODEL), lambda i, c: (0, c)),
            pl.BlockSpec((n_groups, 1, D_MODEL), lambda i, c: (0, 0, 0)),
            table, table,
            pl.BlockSpec((MXU_N, MXU_N), lambda i, c: (0, 0)),
        ] + [pl.BlockSpec((tm, tm), lambda i, c: (0, 0))] * len(perms),
        out_specs=out_specs,
        out_shape=out_shape,
        scratch_shapes=[pltpu.VMEM((n_groups, tm, D_MODEL), BF16)],
        compiler_params=_params("arbitrary", "arbitrary"),
        name="norm_proj_tails" if tails else "norm_proj_rotary",
    )(x, g, w, gain, cos, sin, e, *perms)
    if tails:
        return list(outs)
    return [o.reshape(M // seq * rs[cs // gdiv], seq // rs[cs // gdiv], D_MODEL) for cs, o in enumerate(outs)]


def _rope_tables(pos):
    half = HEAD_DIM // 2
    inv = ROPE_THETA ** (-jnp.arange(half, dtype=F32) / half)
    ang = pos.astype(F32)[:, None] * inv[None, :]
    cos, sin = jnp.cos(ang), jnp.sin(ang)
    reps = LANES // HEAD_DIM
    return jnp.tile(cos, (1, 2 * reps)), jnp.tile(jnp.concatenate([-sin, sin], axis=1), (1, reps))


def _qk(q, k):
    return lax.dot_general(q, k, (((1,), (1,)), ((), ())), preferred_element_type=F32)


def _head_masks(rows):
    first = lax.broadcasted_iota(jnp.int32, (rows, LANES), 1) < HEAD_DIM
    return first


def _band_attn_kernel(q_ref, k_ref, v_ref, o_ref, lse_ref, kf_ref, vf_ref, s_ref, p_ref, *, tq):
    n = pl.program_id(2)
    rows = o_ref.shape[3]

    def store(ref, i, sl, val):
        if rows >= BLK:
            per = rows // BLK
            ref[0, i // per, 0, pl.ds(pl.multiple_of((i % per) * BLK, BLK), BLK), sl] = val
        else:
            for u in range(BLK // rows):
                ref[0, i * (BLK // rows) + u, 0, :, sl] = val[u * rows:(u + 1) * rows]

    @pl.when(n == 0)
    def _():
        kf_ref[0:BLK, :] = jnp.zeros((BLK, D_MODEL), BF16)
        vf_ref[0:BLK, :] = jnp.zeros((BLK, D_MODEL), BF16)

    @pl.when(n > 0)
    def _():
        kf_ref[0:BLK, :] = kf_ref[tq:tq + BLK, :]
        vf_ref[0:BLK, :] = vf_ref[tq:tq + BLK, :]

    kf_ref[BLK:BLK + tq, :] = k_ref[0]
    vf_ref[BLK:BLK + tq, :] = v_ref[0]

    qi = lax.broadcasted_iota(jnp.int32, (BLK, 2 * BLK), 0)
    ki = lax.broadcasted_iota(jnp.int32, (BLK, 2 * BLK), 1)
    dist = BLK + qi - ki
    band = (dist >= 0) & (dist <= BLK)
    first = _head_masks(BLK)
    lane = lax.broadcasted_iota(jnp.int32, (BLK, LANES), 1)
    zero = jnp.zeros((BLK, LANES), BF16)

    def block(i, carry):
        r0 = pl.multiple_of(i * BLK, BLK)
        mask = band & ((ki >= BLK) | (n + i > 0))
        for p in range(N_HEADS // 2):
            sl = slice(p * LANES, (p + 1) * LANES)
            qp = q_ref[0, pl.ds(r0, BLK), sl]
            kp = kf_ref[pl.ds(r0, 2 * BLK), sl]
            s_ref[2 * p] = jnp.where(mask, _qk(jnp.where(first, qp, zero), kp), -jnp.inf)
            s_ref[2 * p + 1] = jnp.where(mask, _qk(jnp.where(first, zero, qp), kp), -jnp.inf)
        s = s_ref[...]
        m = jnp.max(s, axis=-1, keepdims=True)
        e = jnp.exp(s - m)
        l = jnp.sum(e, axis=-1, keepdims=True)
        p_ref[...] = e.astype(BF16)
        inv = 1.0 / l
        lse = m + jnp.log(l)
        lse_tile = jnp.zeros((BLK, LANES), F32)
        for p in range(N_HEADS // 2):
            sl = slice(p * LANES, (p + 1) * LANES)
            vp = vf_ref[pl.ds(r0, 2 * BLK), sl]
            oa = jnp.dot(p_ref[2 * p], vp, preferred_element_type=F32) * inv[2 * p]
            ob = jnp.dot(p_ref[2 * p + 1], vp, preferred_element_type=F32) * inv[2 * p + 1]
            store(o_ref, i, sl, jnp.where(first, oa, ob).astype(BF16))
            lse_tile = jnp.where(lane == 2 * p, lse[2 * p], lse_tile)
            lse_tile = jnp.where(lane == 2 * p + 1, lse[2 * p + 1], lse_tile)
        store(lse_ref, i, slice(None), lse_tile)
        return carry

    lax.fori_loop(0, tq // BLK, block, 0)


def _band_attention(q, k, v, g, *, B, S, tm):
    r = DILATIONS[g]
    L = S // r
    tq = min(L, 4 * BLK)
    rows = tm // r
    nt = tq // rows
    spec = pl.BlockSpec((1, tq, D_MODEL), lambda b, c, n: (b * r + c, n, 0))
    o, lse = pl.pallas_call(
        functools.partial(_band_attn_kernel, tq=tq),
        grid=(B, r, L // tq),
        in_specs=[spec, spec, spec],
        out_specs=[
            pl.BlockSpec((1, nt, 1, rows, D_MODEL), lambda b, c, n: (b, n, c, 0, 0)),
            pl.BlockSpec((1, nt, 1, rows, LANES), lambda b, c, n: (b, n, c, 0, 0)),
        ],
        out_shape=[jax.ShapeDtypeStruct((B, S // tm, r, rows, D_MODEL), BF16),
                   jax.ShapeDtypeStruct((B, S // tm, r, rows, LANES), F32)],
        scratch_shapes=[pltpu.VMEM((BLK + tq, D_MODEL), BF16), pltpu.VMEM((BLK + tq, D_MODEL), BF16),
                        pltpu.VMEM((N_HEADS, BLK, 2 * BLK), F32), pltpu.VMEM((N_HEADS, BLK, 2 * BLK), BF16)],
        compiler_params=_params("parallel", "parallel", "arbitrary"),
        name=f"band_attention_g{g}",
    )(q, k, v)
    return o.reshape(B * S // tm, r, rows, D_MODEL), lse.reshape(B * S // tm, r, rows, LANES)


NEW_PAD = LANES


def _cache_attn_kernel(*refs, lb, r, t, hb, roll):
    if roll:
        q_ref, kn_ref, vn_ref, newt_ref, c_ref, o_ref, lse_ref, roll_ref, new_ref = refs
    else:
        q_ref, kn_ref, vn_ref, c_ref, o_ref, lse_ref, new_ref = refs
    j = pl.program_id(1)
    cols = hb * HEAD_DIM
    new_ref[...] = jnp.zeros((NEW_PAD, 2 * cols), BF16)
    new_ref[0:t, 0:cols] = kn_ref[0]
    new_ref[0:t, cols:2 * cols] = vn_ref[0]

    ti = lax.broadcasted_iota(jnp.int32, (t, NEW_PAD + lb), 0)
    rho = lax.broadcasted_iota(jnp.int32, (t, NEW_PAD + lb), 1)
    idx = jnp.where(rho < NEW_PAD, lb + rho, rho - NEW_PAD)
    d = lb + ti - idx
    mask = ((rho < t) | (rho >= NEW_PAD)) & (d >= 0) & (d <= BLK * r) & ((d & (r - 1)) == 0)
    first = _head_masks(t)
    lane = lax.broadcasted_iota(jnp.int32, (t, LANES), 1)
    zero = jnp.zeros((t, LANES), BF16)

    lse_tile = jnp.zeros((t, LANES), F32)
    for p in range(hb // 2):
        ks = slice(p * LANES, (p + 1) * LANES)
        vs = slice(cols + p * LANES, cols + (p + 1) * LANES)
        qp = q_ref[0, :, ks]
        k2 = c_ref[0, 0, 2 * p:2 * p + 2].reshape(LANES, lb).astype(BF16)
        v2 = c_ref[0, 1, 2 * p:2 * p + 2].reshape(LANES, lb).astype(BF16)
        outs = []
        for hh, qm in enumerate((jnp.where(first, qp, zero), jnp.where(first, zero, qp))):
            s = jnp.concatenate([_qk(qm, new_ref[:, ks]), jnp.dot(qm, k2, preferred_element_type=F32)], axis=1)
            s = jnp.where(mask, s, -jnp.inf)
            m = jnp.max(s, axis=-1, keepdims=True)
            e = jnp.exp(s - m)
            l = jnp.sum(e, axis=-1, keepdims=True)
            eb = e.astype(BF16)
            o = (jnp.dot(eb[:, :NEW_PAD], new_ref[:, vs], preferred_element_type=F32)
                 + _qk(eb[:, NEW_PAD:], v2)) * (1.0 / l)
            outs.append(o)
            lse_tile = jnp.where(lane == j * hb + 2 * p + hh, m + jnp.log(l), lse_tile)
        o_ref[0, :, ks] = jnp.where(first, outs[0], outs[1])
    lse_ref[0, 0] = lse_tile

    if roll:
        for kv in range(2):
            for h in range(hb):
                roll_ref[0, kv, h] = pltpu.roll(c_ref[0, kv, h], lb - t, 1)
                roll_ref[0, kv, h, :, lb - t:lb] = newt_ref[0, kv, h]


CACHE_BLOCK_BYTES = 8 * 1024 * 1024


def _cache_attention(q, kn, vn, cache_t, g, new_t=None):
    B, T, _ = q.shape
    lb = cache_t.shape[-1]
    r = DILATIONS[g]
    hb = N_HEADS
    while 2 * hb * HEAD_DIM * lb * 4 > CACHE_BLOCK_BYTES:
        hb //= 2
    nj = N_HEADS // hb
    roll = new_t is not None
    new = pl.BlockSpec((1, T, hb * HEAD_DIM), lambda b, j: (b, 0, j))
    heads = lambda w: pl.BlockSpec((1, 2, hb, HEAD_DIM, w), lambda b, j: (b, 0, j, 0, 0))
    out_specs = [new, pl.BlockSpec((1, 1, T, LANES), lambda b, j: (b, j, 0, 0))]
    out_shape = [jax.ShapeDtypeStruct((B, T, D_MODEL), F32), jax.ShapeDtypeStruct((B, nj, T, LANES), F32)]
    if roll:
        out_specs.append(heads(lb))
        out_shape.append(jax.ShapeDtypeStruct(cache_t.shape, cache_t.dtype))
    o, lse, *rolled = pl.pallas_call(
        functools.partial(_cache_attn_kernel, lb=lb, r=r, t=T, hb=hb, roll=roll),
        grid=(B, nj),
        in_specs=[new, new, new] + ([heads(T)] if roll else []) + [heads(lb)],
        out_specs=out_specs,
        out_shape=out_shape,
        scratch_shapes=[pltpu.VMEM((NEW_PAD, 2 * hb * HEAD_DIM), BF16)],
        compiler_params=_params("parallel", "parallel"),
        name=f"cache_attention_g{g}",
    )(q, kn, vn, *([new_t] if roll else []), cache_t)
    return o.reshape(B * T, D_MODEL), jnp.sum(lse, axis=1).reshape(B * T, LANES), rolled


def _merge_kernel(*refs, rs):
    n = len(rs)
    x_ref = refs[0]
    o_refs = refs[1:1 + n]
    l_refs = refs[1 + n:1 + 2 * n]
    pt_refs = refs[1 + 2 * n:-2]
    wo_ref, y_ref = refs[-2:]
    tm = x_ref.shape[0]
    os_, ls = [], []
    k = 0
    for o_ref, l_ref, r in zip(o_refs, l_refs, rs):
        o = o_ref[0].reshape(tm, D_MODEL)
        l = l_ref[0].reshape(tm, LANES)
        if r > 1:
            o, l = _in_order(pt_refs[k][...], o), _in_order(pt_refs[k][...], l)
            k += 1
        os_.append(o)
        ls.append(l)
    m = functools.reduce(jnp.maximum, ls)
    es = [jnp.exp(l - m) for l in ls]
    inv = 1.0 / functools.reduce(lambda a, b: a + b, es)
    ws = [e * inv for e in es]
    first = _head_masks(tm)
    cols = []
    for p in range(N_HEADS // 2):
        sl = slice(p * LANES, (p + 1) * LANES)
        acc = None
        for w, o in zip(ws, os_):
            wp = jnp.where(first, w[:, 2 * p:2 * p + 1], w[:, 2 * p + 1:2 * p + 2])
            term = wp * o[:, sl].astype(F32)
            acc = term if acc is None else acc + term
        cols.append(acc.astype(BF16))
    merged = jnp.concatenate(cols, axis=-1)
    y_ref[...] = x_ref[...] + jnp.dot(merged, wo_ref[...], preferred_element_type=F32)


def _merge_layer(x, outs, lses, wo, *, tm, rs):
    M = x.shape[0]
    row = lambda w: pl.BlockSpec((tm, w), lambda i: (i, 0))
    part = lambda r, w: pl.BlockSpec((1, r, tm // r, w), lambda i: (i, 0, 0, 0))
    n_of = jnp.arange(tm, dtype=jnp.int32)
    pts = [(n_of[:, None] == _residue_major(tm, r)[None, :]).astype(BF16) for r in rs if r > 1]
    return pl.pallas_call(
        functools.partial(_merge_kernel, rs=tuple(rs)),
        grid=(M // tm,),
        in_specs=([row(D_MODEL)] + [part(r, D_MODEL) for r in rs] + [part(r, LANES) for r in rs]
                  + [pl.BlockSpec((tm, tm), lambda i: (0, 0))] * len(pts)
                  + [pl.BlockSpec((D_MODEL, D_MODEL), lambda i: (0, 0))]),
        out_specs=row(D_MODEL),
        out_shape=jax.ShapeDtypeStruct((M, D_MODEL), F32),
        compiler_params=_params("parallel"),
        name="merge_groups_wo",
    )(x, *outs, *lses, *pts, wo)


def _seq_minor(a):
    return jnp.transpose(a, (0, 2, 3, 4, 1))


def _seq_major(a):
    return jnp.transpose(a, (0, 4, 1, 2, 3))


def _trunk(x, pos, conv_prev, kv_caches, w, *, bb, ts, tm):
    B, S, _ = x.shape
    M = B * S
    cos, sin = _rope_tables(pos)
    if S < tm:
        cos, sin = jnp.tile(cos, (tm // S, 1)), jnp.tile(sin, (tm // S, 1))
    prev32 = jnp.pad(conv_prev, ((0, 0), (0, 0), (CONV_HALO - (CONV_WIDTH - 1), 0), (0, 0)))

    rs = DILATIONS if kv_caches is None else (1,) * N_GROUPS
    seq = S if kv_caches is None else M
    conv_states = []
    new_bufs = []
    kv = None
    for layer in range(DEPTH):
        if layer < N_A:
            x, st = _conv_layer(x, prev32[layer], w['a_norm'][layer], w['a_w_in'][layer], w['a_b_in'][layer],
                                w['a_w_dw'][layer], w['a_b_dw'][layer], w['a_ln_g'][layer], w['a_ln_b'][layer],
                                w['a_w_out'][layer], w['a_b_out'][layer], bb=bb, ts=ts)
            conv_states.append(st[:, CONV_HALO - (CONV_WIDTH - 1):])
            xf = x.reshape(M, D_MODEL)
        else:
            j = layer - N_A
            if layer == N_A:
                tails = tuple(min(wd, S) for wd in WINDOWS) if kv_caches is None else (M,) * N_GROUPS
                kv_args = (xf, w['kv_norm'], w['w_kv'], w['k_gain'], cos, sin, w['head_ones'])
                kv = _proj(*kv_args, tm=tm, plain_every=2, cps=2, rs=rs, seq=seq)
                tails_f32 = [t.reshape(B, -1, 2, N_HEADS, HEAD_DIM) for t in
                             _proj(*kv_args, tm=tm, plain_every=2, cps=2, rs=(1,) * N_GROUPS, seq=seq, tails=tails)]
                if kv_caches is None:
                    new_bufs = tails_f32
            q = _proj(xf, w['b_norm'][j], w['w_q'][j], w['q_gain'][j], cos, sin, w['head_ones'],
                      tm=tm, plain_every=0, cps=N_GROUPS, rs=rs, seq=seq)
            outs, lses = [], []
            for g in range(N_GROUPS):
                if kv_caches is None:
                    o, l = _band_attention(q[g], kv[2 * g], kv[2 * g + 1], g, B=B, S=S, tm=tm)
                else:
                    new = lambda a: a.reshape(B, S, D_MODEL)
                    new_t = _seq_minor(tails_f32[g]) if layer == N_A else None
                    o, l, rolled = _cache_attention(new(q[g]), new(kv[2 * g]), new(kv[2 * g + 1]),
                                                    _seq_minor(kv_caches[g]), g, new_t)
                    if rolled:
                        new_bufs.append(_seq_major(rolled[0]))
                    o, l = o.reshape(M // tm, 1, tm, D_MODEL), l.reshape(M // tm, 1, tm, LANES)
                outs.append(o)
                lses.append(l)
            xf = _merge_layer(xf, outs, lses, w['w_o'][j], tm=tm, rs=rs)
        xf = _ffn_layer(xf, w['ffn_norm'][layer], w['w_gate_up'][layer], w['w_down'][layer], tm=tm)
        x = xf.reshape(B, S, D_MODEL)
    return x, jnp.stack(conv_states, axis=0), new_bufs


def kernel(x_prompt, x_sample, cache_conv, cache_kv_w128, cache_kv_w512, cache_kv_w2048,
           a_norm, a_w_in, a_b_in, a_w_dw, a_b_dw, a_ln_g, a_ln_b, a_w_out, a_b_out,
           kv_norm, w_kv, k_norm, b_norm, w_q, q_norm, w_o, ffn_norm, w_gate_up, w_down):
    B, S, _ = x_prompt.shape
    Bd, T, _ = x_sample.shape
    row = lambda a: a.reshape(a.shape[:-1] + (1, a.shape[-1]))
    head_of = jnp.arange(MXU_N, dtype=jnp.int32) // HEAD_DIM
    w = dict(
        a_norm=row(a_norm), a_w_in=a_w_in.astype(BF16), a_b_in=row(a_b_in), a_w_dw=a_w_dw, a_b_dw=row(a_b_dw),
        a_ln_g=row(a_ln_g), a_ln_b=row(a_ln_b), a_w_out=a_w_out.astype(BF16), a_b_out=row(a_b_out),
        kv_norm=row(kv_norm), w_kv=w_kv.astype(BF16), b_norm=row(b_norm), w_q=w_q.astype(BF16),
        w_o=w_o.astype(BF16), ffn_norm=row(ffn_norm), w_gate_up=w_gate_up.astype(BF16), w_down=w_down.astype(BF16),
        k_gain=row(jnp.tile(k_norm, (1, N_HEADS))),
        q_gain=row(jnp.tile(q_norm, (1, 1, N_HEADS)) * SCALE),
        head_ones=(head_of[:, None] == head_of[None, :]).astype(BF16),
    )
    pos_prompt = jnp.arange(S, dtype=jnp.int32)
    pos_sample = PAST_LEN + jnp.arange(T, dtype=jnp.int32)
    conv_zero = jnp.zeros((N_A, B, CONV_WIDTH - 1, D_MODEL), x_prompt.dtype)
    y_p, conv_p, bufs_p = _trunk(x_prompt, pos_prompt, conv_zero, None, w, bb=1, ts=512, tm=512)
    y_s, conv_s, bufs_s = _trunk(x_sample, pos_sample, cache_conv,
                                 (cache_kv_w128, cache_kv_w512, cache_kv_w2048), w, bb=Bd, ts=T, tm=Bd * T)
    return (y_p, y_s, conv_p, conv_s, bufs_p[0], bufs_p[1], bufs_p[2], bufs_s[0], bufs_s[1], bufs_s[2])
```

```python
import functools

import jax
import jax.numpy as jnp
from jax import lax
from jax.experimental import pallas as pl
from jax.experimental.pallas import tpu as pltpu

D_MODEL = 1024
DEPTH = 4
PAST_LEN = 8192
N_A = DEPTH // 2
N_B = DEPTH - N_A
CONV_WIDTH = 31
CONV_HALO = 32
SUBLANES = 8
LANES = 128
N_HEADS = 16
HEAD_DIM = D_MODEL // N_HEADS
WINDOWS = (128, 512, 2048)
DILATIONS = (1, 4, 16)
N_GROUPS = len(WINDOWS)
BLK = 128
D_FF = (8 * D_MODEL + 3 * 256 - 1) // (3 * 256) * 256
ROPE_THETA = 10000.0
EPS = 1e-6
SCALE = HEAD_DIM ** -0.5
MXU_N = 256
PROJ_SLAB = 2 * MXU_N
VMEM_LIMIT = 56 * 1024 * 1024

F32 = jnp.float32
BF16 = jnp.bfloat16


def _params(*sem):
    return pltpu.CompilerParams(dimension_semantics=sem, vmem_limit_bytes=VMEM_LIMIT)


def _rms(x, g):
    return x * lax.rsqrt(jnp.mean(x * x, axis=-1, keepdims=True) + EPS) * g


def _sigmoid(x):
    return 1.0 / (1.0 + jnp.exp(-x))


def _conv_kernel(x_ref, prev_ref, ng_ref, win_ref, bin_ref, wdw_ref, bdw_ref, lng_ref, lnb_ref,
                 wout_ref, bout_ref, y_ref, st_ref, ext_ref, *, ts):
    s = pl.program_id(1)
    bb = x_ref.shape[0]

    @pl.when(s == 0)
    def _():
        ext_ref[:, 0:CONV_HALO, :] = prev_ref[...]

    @pl.when(s > 0)
    def _():
        ext_ref[:, 0:CONV_HALO, :] = ext_ref[:, ts:ts + CONV_HALO, :]

    x = x_ref[...].reshape(bb * ts, D_MODEL)
    u = _rms(x, ng_ref[...]).astype(BF16)
    a = jnp.dot(u, win_ref[...], preferred_element_type=F32) + bin_ref[...]
    glu = a[:, :D_MODEL] * _sigmoid(a[:, D_MODEL:])
    ext_ref[:, CONV_HALO:CONV_HALO + ts, :] = glu.reshape(bb, ts, D_MODEL)

    off = CONV_HALO - (CONV_WIDTH - 1)
    c = None
    for b in range(SUBLANES):
        rows = ts if b == 0 else ts + SUBLANES
        yb = None
        for a_ in range((off + CONV_WIDTH - 1) // SUBLANES + 1):
            j = SUBLANES * a_ + b - off
            if 0 <= j < CONV_WIDTH:
                term = ext_ref[:, SUBLANES * a_:SUBLANES * a_ + rows, :] * wdw_ref[j:j + 1, :]
                yb = term if yb is None else yb + term
        yb = yb[:, b:b + ts, :]
        c = yb if c is None else c + yb
    c = c.reshape(bb * ts, D_MODEL) + bdw_ref[...]
    mu = jnp.mean(c, axis=-1, keepdims=True)
    cc = c - mu
    yn = cc * lax.rsqrt(jnp.mean(cc * cc, axis=-1, keepdims=True) + EPS) * lng_ref[...] + lnb_ref[...]
    h = (yn * _sigmoid(yn)).astype(BF16)
    out = x + jnp.dot(h, wout_ref[...], preferred_element_type=F32) + bout_ref[...]
    y_ref[...] = out.reshape(bb, ts, D_MODEL)
    st_ref[...] = ext_ref[:, ts:ts + CONV_HALO, :]


def _conv_layer(x, prev32, ng, win, bin_, wdw, bdw, lng, lnb, wout, bout, *, bb, ts):
    B, S, _ = x.shape
    full = lambda shape: pl.BlockSpec(shape, lambda b, s: (0,) * len(shape))
    return pl.pallas_call(
        functools.partial(_conv_kernel, ts=ts),
        grid=(B // bb, S // ts),
        in_specs=[
            pl.BlockSpec((bb, ts, D_MODEL), lambda b, s: (b, s, 0)),
            pl.BlockSpec((bb, CONV_HALO, D_MODEL), lambda b, s: (b, 0, 0)),
            full((1, D_MODEL)), full((D_MODEL, 2 * D_MODEL)), full((1, 2 * D_MODEL)),
            full((CONV_WIDTH, D_MODEL)), full((1, D_MODEL)), full((1, D_MODEL)), full((1, D_MODEL)),
            full((D_MODEL, D_MODEL)), full((1, D_MODEL)),
        ],
        out_specs=[
            pl.BlockSpec((bb, ts, D_MODEL), lambda b, s: (b, s, 0)),
            pl.BlockSpec((bb, CONV_HALO, D_MODEL), lambda b, s: (b, 0, 0)),
        ],
        out_shape=[jax.ShapeDtypeStruct((B, S, D_MODEL), F32),
                   jax.ShapeDtypeStruct((B, CONV_HALO, D_MODEL), F32)],
        scratch_shapes=[pltpu.VMEM((bb, CONV_HALO + ts, D_MODEL), F32)],
        compiler_params=_params("parallel", "arbitrary"),
        name="conv_module",
    )(x, prev32, ng, win, bin_, wdw, bdw, lng, lnb, wout, bout)


FF_SLAB = 256


def _ffn_kernel(x_ref, g_ref, wgu_ref, wd_ref, y_ref, h_ref):
    x = x_ref[...]
    xn = _rms(x, g_ref[...]).astype(BF16)
    for k in range(D_FF // FF_SLAB):
        gate = jnp.dot(xn, wgu_ref[:, k * FF_SLAB:(k + 1) * FF_SLAB], preferred_element_type=F32)
        up = jnp.dot(xn, wgu_ref[:, D_FF + k * FF_SLAB:D_FF + (k + 1) * FF_SLAB], preferred_element_type=F32)
        h_ref[:, k * FF_SLAB:(k + 1) * FF_SLAB] = (gate * _sigmoid(gate) * up).astype(BF16)
    y_ref[...] = x + jnp.dot(h_ref[...], wd_ref[...], preferred_element_type=F32)


def _ffn_layer(x, g, wgu, wd, *, tm):
    M = x.shape[0]
    resident = lambda shape: pl.BlockSpec(shape, lambda i: (0, 0), pipeline_mode=pl.Buffered(1))
    return pl.pallas_call(
        _ffn_kernel,
        grid=(M // tm,),
        in_specs=[
            pl.BlockSpec((tm, D_MODEL), lambda i: (i, 0)),
            pl.BlockSpec((1, D_MODEL), lambda i: (0, 0)),
            resident((D_MODEL, 2 * D_FF)),
            resident((D_FF, D_MODEL)),
        ],
        out_specs=pl.BlockSpec((tm, D_MODEL), lambda i: (i, 0)),
        out_shape=jax.ShapeDtypeStruct((M, D_MODEL), F32),
        scratch_shapes=[pltpu.VMEM((tm, D_FF), BF16)],
        compiler_params=_params("parallel"),
        name="swiglu_ffn",
    )(x, g, wgu, wd)


def _normed_slab(y, gain, cos, sin, e):
    ssq = jnp.dot((y * y).astype(BF16), e, preferred_element_type=F32)
    yn = y * lax.rsqrt(ssq * (1.0 / HEAD_DIM) + EPS) * gain
    lane = lax.broadcasted_iota(jnp.int32, (y.shape[0], LANES), 1)
    lo_half = (lane % HEAD_DIM) < (HEAD_DIM // 2)
    outs = []
    for k in range(MXU_N // LANES):
        z = yn[:, k * LANES:(k + 1) * LANES]
        partner = jnp.where(lo_half, pltpu.roll(z, LANES - HEAD_DIM // 2, 1), pltpu.roll(z, HEAD_DIM // 2, 1))
        outs.append(z * cos + partner * sin)
    return jnp.concatenate(outs, axis=-1)


def _in_order(pt, v):
    if v.dtype == BF16:
        return jnp.dot(pt, v, preferred_element_type=F32)
    hi = v.astype(BF16)
    rest = v - hi.astype(F32)
    mid = rest.astype(BF16)
    lo = (rest - mid.astype(F32)).astype(BF16)
    return (jnp.dot(pt, hi, preferred_element_type=F32) + jnp.dot(pt, mid, preferred_element_type=F32)
            + jnp.dot(pt, lo, preferred_element_type=F32))


def _proj_kernel(*refs, tm, n_chunks, cps, plain_every, tails, n_s, rs):
    x_ref, g_ref, w_ref, gain_ref, cos_ref, sin_ref, e_ref = refs[:7]
    perm_of = {g: k for k, g in enumerate(g for g, r in enumerate(rs) if r > 1)}
    n_perm = len(perm_of)
    perm_refs = refs[7:7 + n_perm]
    out_refs = refs[7 + n_perm:-1]
    ob_refs, tail_refs = ((), out_refs) if tails else (out_refs, ())
    xn_ref = refs[-1]
    i = pl.program_id(0)
    c = pl.program_id(1)
    s = i % n_s

    @pl.when(c == 0)
    def _():
        xn = _rms(x_ref[...], g_ref[...]).astype(BF16)
        for g, r in enumerate(rs):
            if r == 1:
                xn_ref[g] = xn
            else:
                xn_ref[g] = jnp.dot(perm_refs[perm_of[g]][...], xn, preferred_element_type=F32).astype(BF16)

    def chunk(g, w0, normed, store):
        xn = xn_ref[g]
        for k in range(D_MODEL // PROJ_SLAB):
            y = jnp.dot(xn, w_ref[:, w0 + k * PROJ_SLAB:w0 + (k + 1) * PROJ_SLAB], preferred_element_type=F32)
            for h in range(PROJ_SLAB // MXU_N):
                sl = slice(k * PROJ_SLAB + h * MXU_N, k * PROJ_SLAB + (h + 1) * MXU_N)
                z = y[:, h * MXU_N:(h + 1) * MXU_N]
                if normed:
                    z = _normed_slab(z, gain_ref[g, :, sl], cos_ref[g], sin_ref[g], e_ref[...])
                store(sl, z)

    def step(st):
        for u in range(cps):
            cs = st * cps + u
            g, part = divmod(cs, max(plain_every, 1))
            if not tails:
                def store(sl, z, ob_ref=ob_refs[cs], r=rs[g]):
                    zb = z.astype(BF16)
                    rows = tm // r
                    for cc in range(r):
                        ob_ref[0, cc, 0, :, sl] = zb[cc * rows:(cc + 1) * rows]
                chunk(g, u * D_MODEL, part == 0, store)
            else:
                lg = tails[g]
                keep = min(lg, tm)
                def store(sl, z, tail_ref=tail_refs[g], col0=part * D_MODEL, keep=keep):
                    tail_ref[0, :, col0 + sl.start:col0 + sl.stop] = z[tm - keep:, :]
                pl.when(s >= n_s - max(lg // tm, 1))(functools.partial(chunk, g, u * D_MODEL, part == 0, store))

    if n_chunks == cps:
        step(0)
    else:
        for st in range(n_chunks // cps):
            pl.when(c == st)(functools.partial(step, st))


def _residue_major(n, r):
    pos = jnp.arange(n, dtype=jnp.int32)
    return (pos % (n // r)) * r + pos // (n // r)


def _proj(x, g, w, gain, cos, sin, e, *, tm, plain_every, cps, rs, seq=None, tails=()):
    M = x.shape[0]
    n_chunks = w.shape[1] // D_MODEL
    gdiv = max(plain_every, 1)
    seq = seq or M
    n_groups = len(rs)
    n_seq_tiles = seq // tm
    n_pos_tiles = cos.shape[0] // tm
    n_s = max(-(-max(tails) // tm), 1) if tails else n_seq_tiles
    skip = n_seq_tiles - n_s
    orders = [_residue_major(tm, r) for r in rs]
    n_of = jnp.arange(tm, dtype=jnp.int32)
    perms = [(o[:, None] == n_of[None, :]).astype(BF16) for o, r in zip(orders, rs) if r > 1]
    tile_order = lambda tab, o: tab.reshape(n_pos_tiles, tm, LANES)[:, o].reshape(n_pos_tiles * tm, LANES)
    cos = jnp.stack([tile_order(cos, o) for o in orders])
    sin = jnp.stack([tile_order(sin, o) for o in orders])
    out_shape, out_specs = [], []
    if not tails:
        for cs in range(n_chunks):
            r = rs[cs // gdiv]
            out_shape.append(jax.ShapeDtypeStruct((M // seq, r, n_s, tm // r, D_MODEL), BF16))
            out_specs.append(pl.BlockSpec((1, r, 1, tm // r, D_MODEL), lambda i, c: (i // n_s, 0, i % n_s, 0, 0)))
    for lg in tails:
        assert all(r == 1 for r in rs)
        out_shape.append(jax.ShapeDtypeStruct((M // seq, lg, 2 * D_MODEL), F32))
        first = n_s - max(lg // tm, 1)
        out_specs.append(pl.BlockSpec((1, min(lg, tm), 2 * D_MODEL),
                                      lambda i, c, first=first: (i // n_s, jnp.maximum(i % n_s - first, 0), 0)))
    table = pl.BlockSpec((n_groups, tm, LANES), lambda i, c: (0, (skip + i % n_s) % n_pos_tiles, 0))
    outs = pl.pallas_call(
        functools.partial(_proj_kernel, tm=tm, n_chunks=n_chunks, cps=cps, plain_every=plain_every,
                          tails=tuple(tails), n_s=n_s, rs=tuple(rs)),
        grid=(M // seq * n_s, n_chunks // cps),
        in_specs=[
            pl.BlockSpec((tm, D_MODEL), lambda i, c: (i // n_s * n_seq_tiles + skip + i % n_s, 0)),
            pl.BlockSpec((1, D_MODEL), lambda i, c: (0, 0)),
            pl.BlockSpec((D_MODEL, cps * D_MODEL), lambda i, c: (0, c)),
            pl.BlockSpec((n_groups, 1, D_MODEL), lambda i, c: (0, 0, 0)),
            table, table,
            pl.BlockSpec((MXU_N, MXU_N), lambda i, c: (0, 0)),
        ] + [pl.BlockSpec((tm, tm), lambda i, c: (0, 0))] * len(perms),
        out_specs=out_specs,
        out_shape=out_shape,
        scratch_shapes=[pltpu.VMEM((n_groups, tm, D_MODEL), BF16)],
        compiler_params=_params("arbitrary", "arbitrary"),
        name="norm_proj_tails" if tails else "norm_proj_rotary",
    )(x, g, w, gain, cos, sin, e, *perms)
    if tails:
        return list(outs)
    return [o.reshape(M // seq * rs[cs // gdiv], seq // rs[cs // gdiv], D_MODEL) for cs, o in enumerate(outs)]


def _rope_tables(pos):
    half = HEAD_DIM // 2
    inv = ROPE_THETA ** (-jnp.arange(half, dtype=F32) / half)
    ang = pos.astype(F32)[:, None] * inv[None, :]
    cos, sin = jnp.cos(ang), jnp.sin(ang)
    reps = LANES // HEAD_DIM
    return jnp.tile(cos, (1, 2 * reps)), jnp.tile(jnp.concatenate([-sin, sin], axis=1), (1, reps))


def _qk(q, k):
    return lax.dot_general(q, k, (((1,), (1,)), ((), ())), preferred_element_type=F32)


def _head_masks(rows):
    first = lax.broadcasted_iota(jnp.int32, (rows, LANES), 1) < HEAD_DIM
    return first


def _band_attn_kernel(q_ref, k_ref, v_ref, o_ref, lse_ref, kf_ref, vf_ref, s_ref, p_ref, *, tq):
    n = pl.program_id(2)
    rows = o_ref.shape[3]

    def store(ref, i, sl, val):
        if rows >= BLK:
            per = rows // BLK
            ref[0, i // per, 0, pl.ds(pl.multiple_of((i % per) * BLK, BLK), BLK), sl] = val
        else:
            for u in range(BLK // rows):
                ref[0, i * (BLK // rows) + u, 0, :, sl] = val[u * rows:(u + 1) * rows]

    @pl.when(n == 0)
    def _():
        kf_ref[0:BLK, :] = jnp.zeros((BLK, D_MODEL), BF16)
        vf_ref[0:BLK, :] = jnp.zeros((BLK, D_MODEL), BF16)

    @pl.when(n > 0)
    def _():
        kf_ref[0:BLK, :] = kf_ref[tq:tq + BLK, :]
        vf_ref[0:BLK, :] = vf_ref[tq:tq + BLK, :]

    kf_ref[BLK:BLK + tq, :] = k_ref[0]
    vf_ref[BLK:BLK + tq, :] = v_ref[0]

    qi = lax.broadcasted_iota(jnp.int32, (BLK, 2 * BLK), 0)
    ki = lax.broadcasted_iota(jnp.int32, (BLK, 2 * BLK), 1)
    dist = BLK + qi - ki
    band = (dist >= 0) & (dist <= BLK)
    first = _head_masks(BLK)
    lane = lax.broadcasted_iota(jnp.int32, (BLK, LANES), 1)
    zero = jnp.zeros((BLK, LANES), BF16)

    def block(i, carry):
        r0 = pl.multiple_of(i * BLK, BLK)
        mask = band & ((ki >= BLK) | (n + i > 0))
        for p in range(N_HEADS // 2):
            sl = slice(p * LANES, (p + 1) * LANES)
            qp = q_ref[0, pl.ds(r0, BLK), sl]
            kp = kf_ref[pl.ds(r0, 2 * BLK), sl]
            s_ref[2 * p] = jnp.where(mask, _qk(jnp.where(first, qp, zero), kp), -jnp.inf)
            s_ref[2 * p + 1] = jnp.where(mask, _qk(jnp.where(first, zero, qp), kp), -jnp.inf)
        s = s_ref[...]
        m = jnp.max(s, axis=-1, keepdims=True)
        e = jnp.exp(s - m)
        l = jnp.sum(e, axis=-1, keepdims=True)
        p_ref[...] = e.astype(BF16)
        inv = 1.0 / l
        lse = m + jnp.log(l)
        lse_tile = jnp.zeros((BLK, LANES), F32)
        for p in range(N_HEADS // 2):
            sl = slice(p * LANES, (p + 1) * LANES)
            vp = vf_ref[pl.ds(r0, 2 * BLK), sl]
            oa = jnp.dot(p_ref[2 * p], vp, preferred_element_type=F32) * inv[2 * p]
            ob = jnp.dot(p_ref[2 * p + 1], vp, preferred_element_type=F32) * inv[2 * p + 1]
            store(o_ref, i, sl, jnp.where(first, oa, ob).astype(BF16))
            lse_tile = jnp.where(lane == 2 * p, lse[2 * p], lse_tile)
            lse_tile = jnp.where(lane == 2 * p + 1, lse[2 * p + 1], lse_tile)
        store(lse_ref, i, slice(None), lse_tile)
        return carry

    lax.fori_loop(0, tq // BLK, block, 0)


def _band_attention(q, k, v, g, *, B, S, tm):
    r = DILATIONS[g]
    L = S // r
    tq = min(L, 4 * BLK)
    rows = tm // r
    nt = tq // rows
    spec = pl.BlockSpec((1, tq, D_MODEL), lambda b, c, n: (b * r + c, n, 0))
    o, lse = pl.pallas_call(
        functools.partial(_band_attn_kernel, tq=tq),
        grid=(B, r, L // tq),
        in_specs=[spec, spec, spec],
        out_specs=[
            pl.BlockSpec((1, nt, 1, rows, D_MODEL), lambda b, c, n: (b, n, c, 0, 0)),
            pl.BlockSpec((1, nt, 1, rows, LANES), lambda b, c, n: (b, n, c, 0, 0)),
        ],
        out_shape=[jax.ShapeDtypeStruct((B, S // tm, r, rows, D_MODEL), BF16),
                   jax.ShapeDtypeStruct((B, S // tm, r, rows, LANES), F32)],
        scratch_shapes=[pltpu.VMEM((BLK + tq, D_MODEL), BF16), pltpu.VMEM((BLK + tq, D_MODEL), BF16),
                        pltpu.VMEM((N_HEADS, BLK, 2 * BLK), F32), pltpu.VMEM((N_HEADS, BLK, 2 * BLK), BF16)],
        compiler_params=_params("parallel", "parallel", "arbitrary"),
        name=f"band_attention_g{g}",
    )(q, k, v)
    return o.reshape(B * S // tm, r, rows, D_MODEL), lse.reshape(B * S // tm, r, rows, LANES)


NEW_PAD = LANES


def _cache_attn_kernel(*refs, lb, r, t, hb, roll):
    if roll:
        q_ref, kn_ref, vn_ref, newt_ref, c_ref, o_ref, lse_ref, roll_ref, new_ref = refs
    else:
        q_ref, kn_ref, vn_ref, c_ref, o_ref, lse_ref, new_ref = refs
    j = pl.program_id(1)
    cols = hb * HEAD_DIM
    new_ref[...] = jnp.zeros((NEW_PAD, 2 * cols), BF16)
    new_ref[0:t, 0:cols] = kn_ref[0]
    new_ref[0:t, cols:2 * cols] = vn_ref[0]

    ti = lax.broadcasted_iota(jnp.int32, (t, NEW_PAD + lb), 0)
    rho = lax.broadcasted_iota(jnp.int32, (t, NEW_PAD + lb), 1)
    idx = jnp.where(rho < NEW_PAD, lb + rho, rho - NEW_PAD)
    d = lb + ti - idx
    mask = ((rho < t) | (rho >= NEW_PAD)) & (d >= 0) & (d <= BLK * r) & ((d & (r - 1)) == 0)
    first = _head_masks(t)
    lane = lax.broadcasted_iota(jnp.int32, (t, LANES), 1)
    zero = jnp.zeros((t, LANES), BF16)

    lse_tile = jnp.zeros((t, LANES), F32)
    for p in range(hb // 2):
        ks = slice(p * LANES, (p + 1) * LANES)
        vs = slice(cols + p * LANES, cols + (p + 1) * LANES)
        qp = q_ref[0, :, ks]
        k2 = c_ref[0, 0, 2 * p:2 * p + 2].reshape(LANES, lb).astype(BF16)
        v2 = c_ref[0, 1, 2 * p:2 * p + 2].reshape(LANES, lb).astype(BF16)
        outs = []
        for hh, qm in enumerate((jnp.where(first, qp, zero), jnp.where(first, zero, qp))):
            s = jnp.concatenate([_qk(qm, new_ref[:, ks]), jnp.dot(qm, k2, preferred_element_type=F32)], axis=1)
            s = jnp.where(mask, s, -jnp.inf)
            m = jnp.max(s, axis=-1, keepdims=True)
            e = jnp.exp(s - m)
            l = jnp.sum(e, axis=-1, keepdims=True)
            eb = e.astype(BF16)
            o = (jnp.dot(eb[:, :NEW_PAD], new_ref[:, vs], preferred_element_type=F32)
                 + _qk(eb[:, NEW_PAD:], v2)) * (1.0 / l)
            outs.append(o)
            lse_tile = jnp.where(lane == j * hb + 2 * p + hh, m + jnp.log(l), lse_tile)
        o_ref[0, :, ks] = jnp.where(first, outs[0], outs[1])
    lse_ref[0, 0] = lse_tile

    if roll:
        for kv in range(2):
            for h in range(hb):
                roll_ref[0, kv, h] = pltpu.roll(c_ref[0, kv, h], lb - t, 1)
                roll_ref[0, kv, h, :, lb - t:lb] = newt_ref[0, kv, h]


CACHE_BLOCK_BYTES = 8 * 1024 * 1024


def _cache_attention(q, kn, vn, cache_t, g, new_t=None):
    B, T, _ = q.shape
    lb = cache_t.shape[-1]
    r = DILATIONS[g]
    hb = N_HEADS
    while 2 * hb * HEAD_DIM * lb * 4 > CACHE_BLOCK_BYTES:
        hb //= 2
    nj = N_HEADS // hb
    roll = new_t is not None
    new = pl.BlockSpec((1, T, hb * HEAD_DIM), lambda b, j: (b, 0, j))
    heads = lambda w: pl.BlockSpec((1, 2, hb, HEAD_DIM, w), lambda b, j: (b, 0, j, 0, 0))
    out_specs = [new, pl.BlockSpec((1, 1, T, LANES), lambda b, j: (b, j, 0, 0))]
    out_shape = [jax.ShapeDtypeStruct((B, T, D_MODEL), F32), jax.ShapeDtypeStruct((B, nj, T, LANES), F32)]
    if roll:
        out_specs.append(heads(lb))
        out_shape.append(jax.ShapeDtypeStruct(cache_t.shape, cache_t.dtype))
    o, lse, *rolled = pl.pallas_call(
        functools.partial(_cache_attn_kernel, lb=lb, r=r, t=T, hb=hb, roll=roll),
        grid=(B, nj),
        in_specs=[new, new, new] + ([heads(T)] if roll else []) + [heads(lb)],
        out_specs=out_specs,
        out_shape=out_shape,
        scratch_shapes=[pltpu.VMEM((NEW_PAD, 2 * hb * HEAD_DIM), BF16)],
        compiler_params=_params("parallel", "parallel"),
        name=f"cache_attention_g{g}",
    )(q, kn, vn, *([new_t] if roll else []), cache_t)
    return o.reshape(B * T, D_MODEL), jnp.sum(lse, axis=1).reshape(B * T, LANES), rolled


MERGE_SPLIT = 2


def _merge_kernel(*refs, rs):
    n = len(rs)
    x_ref = refs[0]
    o_refs = refs[1:1 + n]
    l_refs = refs[1 + n:1 + 2 * n]
    pt_refs = refs[1 + 2 * n:-2]
    wo_ref, y_ref = refs[-2:]
    tm = x_ref.shape[0]
    rb = tm // MERGE_SPLIT
    first = _head_masks(rb)
    for rows in (slice(b * rb, (b + 1) * rb) for b in range(MERGE_SPLIT)):
        os_, ls = [], []
        k = 0
        for o_ref, l_ref, r in zip(o_refs, l_refs, rs):
            if r > 1:
                pt = pt_refs[k][rows, :]
                o = _in_order(pt, o_ref[0].reshape(tm, D_MODEL))
                l = _in_order(pt, l_ref[0].reshape(tm, LANES))
                k += 1
            else:
                o, l = o_ref[0, 0, rows, :], l_ref[0, 0, rows, :]
            os_.append(o)
            ls.append(l)
        m = functools.reduce(jnp.maximum, ls)
        es = [jnp.exp(l - m) for l in ls]
        inv = 1.0 / functools.reduce(lambda a, b: a + b, es)
        ws = [e * inv for e in es]
        cols = []
        for p in range(N_HEADS // 2):
            sl = slice(p * LANES, (p + 1) * LANES)
            acc = None
            for w, o in zip(ws, os_):
                wp = jnp.where(first, w[:, 2 * p:2 * p + 1], w[:, 2 * p + 1:2 * p + 2])
                term = wp * o[:, sl].astype(F32)
                acc = term if acc is None else acc + term
            cols.append(acc.astype(BF16))
        merged = jnp.concatenate(cols, axis=-1)
        y_ref[rows, :] = x_ref[rows, :] + jnp.dot(merged, wo_ref[...], preferred_element_type=F32)


def _merge_layer(x, outs, lses, wo, *, tm, rs):
    M = x.shape[0]
    row = lambda w: pl.BlockSpec((tm, w), lambda i: (i, 0))
    part = lambda r, w: pl.BlockSpec((1, r, tm // r, w), lambda i: (i, 0, 0, 0))
    n_of = jnp.arange(tm, dtype=jnp.int32)
    pts = [(n_of[:, None] == _residue_major(tm, r)[None, :]).astype(BF16) for r in rs if r > 1]
    return pl.pallas_call(
        functools.partial(_merge_kernel, rs=tuple(rs)),
        grid=(M // tm,),
        in_specs=([row(D_MODEL)] + [part(r, D_MODEL) for r in rs] + [part(r, LANES) for r in rs]
                  + [pl.BlockSpec((tm, tm), lambda i: (0, 0))] * len(pts)
                  + [pl.BlockSpec((D_MODEL, D_MODEL), lambda i: (0, 0))]),
        out_specs=row(D_MODEL),
        out_shape=jax.ShapeDtypeStruct((M, D_MODEL), F32),
        compiler_params=_params("parallel"),
        name="merge_groups_wo",
    )(x, *outs, *lses, *pts, wo)


def _seq_minor(a):
    return jnp.transpose(a, (0, 2, 3, 4, 1))


def _seq_major(a):
    return jnp.transpose(a, (0, 4, 1, 2, 3))


def _trunk(x, pos, conv_prev, kv_caches, w, *, bb, ts, tm):
    B, S, _ = x.shape
    M = B * S
    cos, sin = _rope_tables(pos)
    if S < tm:
        cos, sin = jnp.tile(cos, (tm // S, 1)), jnp.tile(sin, (tm // S, 1))
    prev32 = jnp.pad(conv_prev, ((0, 0), (0, 0), (CONV_HALO - (CONV_WIDTH - 1), 0), (0, 0)))

    rs = DILATIONS if kv_caches is None else (1,) * N_GROUPS
    seq = S if kv_caches is None else M
    conv_states = []
    new_bufs = []
    kv = None
    for layer in range(DEPTH):
        if layer < N_A:
            x, st = _conv_layer(x, prev32[layer], w['a_norm'][layer], w['a_w_in'][layer], w['a_b_in'][layer],
                                w['a_w_dw'][layer], w['a_b_dw'][layer], w['a_ln_g'][layer], w['a_ln_b'][layer],
                                w['a_w_out'][layer], w['a_b_out'][layer], bb=bb, ts=ts)
            conv_states.append(st[:, CONV_HALO - (CONV_WIDTH - 1):])
            xf = x.reshape(M, D_MODEL)
        else:
            j = layer - N_A
            if layer == N_A:
                tails = tuple(min(wd, S) for wd in WINDOWS) if kv_caches is None else (M,) * N_GROUPS
                kv_args = (xf, w['kv_norm'], w['w_kv'], w['k_gain'], cos, sin, w['head_ones'])
                kv = _proj(*kv_args, tm=tm, plain_every=2, cps=2, rs=rs, seq=seq)
                tails_f32 = [t.reshape(B, -1, 2, N_HEADS, HEAD_DIM) for t in
                             _proj(*kv_args, tm=tm, plain_every=2, cps=2, rs=(1,) * N_GROUPS, seq=seq, tails=tails)]
                if kv_caches is None:
                    new_bufs = tails_f32
            q = _proj(xf, w['b_norm'][j], w['w_q'][j], w['q_gain'][j], cos, sin, w['head_ones'],
                      tm=tm, plain_every=0, cps=N_GROUPS, rs=rs, seq=seq)
            outs, lses = [], []
            for g in range(N_GROUPS):
                if kv_caches is None:
                    o, l = _band_attention(q[g], kv[2 * g], kv[2 * g + 1], g, B=B, S=S, tm=tm)
                else:
                    new = lambda a: a.reshape(B, S, D_MODEL)
                    new_t = _seq_minor(tails_f32[g]) if layer == N_A else None
                    o, l, rolled = _cache_attention(new(q[g]), new(kv[2 * g]), new(kv[2 * g + 1]),
                                                    _seq_minor(kv_caches[g]), g, new_t)
                    if rolled:
                        new_bufs.append(_seq_major(rolled[0]))
                    o, l = o.reshape(M // tm, 1, tm, D_MODEL), l.reshape(M // tm, 1, tm, LANES)
                outs.append(o)
                lses.append(l)
            xf = _merge_layer(xf, outs, lses, w['w_o'][j], tm=tm, rs=rs)
        xf = _ffn_layer(xf, w['ffn_norm'][layer], w['w_gate_up'][layer], w['w_down'][layer], tm=tm)
        x = xf.reshape(B, S, D_MODEL)
    return x, jnp.stack(conv_states, axis=0), new_bufs


def kernel(x_prompt, x_sample, cache_conv, cache_kv_w128, cache_kv_w512, cache_kv_w2048,
           a_norm, a_w_in, a_b_in, a_w_dw, a_b_dw, a_ln_g, a_ln_b, a_w_out, a_b_out,
           kv_norm, w_kv, k_norm, b_norm, w_q, q_norm, w_o, ffn_norm, w_gate_up, w_down):
    B, S, _ = x_prompt.shape
    Bd, T, _ = x_sample.shape
    row = lambda a: a.reshape(a.shape[:-1] + (1, a.shape[-1]))
    head_of = jnp.arange(MXU_N, dtype=jnp.int32) // HEAD_DIM
    w = dict(
        a_norm=row(a_norm), a_w_in=a_w_in.astype(BF16), a_b_in=row(a_b_in), a_w_dw=a_w_dw, a_b_dw=row(a_b_dw),
        a_ln_g=row(a_ln_g), a_ln_b=row(a_ln_b), a_w_out=a_w_out.astype(BF16), a_b_out=row(a_b_out),
        kv_norm=row(kv_norm), w_kv=w_kv.astype(BF16), b_norm=row(b_norm), w_q=w_q.astype(BF16),
        w_o=w_o.astype(BF16), ffn_norm=row(ffn_norm), w_gate_up=w_gate_up.astype(BF16), w_down=w_down.astype(BF16),
        k_gain=row(jnp.tile(k_norm, (1, N_HEADS))),
        q_gain=row(jnp.tile(q_norm, (1, 1, N_HEADS)) * SCALE),
        head_ones=(head_of[:, None] == head_of[None, :]).astype(BF16),
    )
    pos_prompt = jnp.arange(S, dtype=jnp.int32)
    pos_sample = PAST_LEN + jnp.arange(T, dtype=jnp.int32)
    conv_zero = jnp.zeros((N_A, B, CONV_WIDTH - 1, D_MODEL), x_prompt.dtype)
    y_p, conv_p, bufs_p = _trunk(x_prompt, pos_prompt, conv_zero, None, w, bb=1, ts=512, tm=512)
    y_s, conv_s, bufs_s = _trunk(x_sample, pos_sample, cache_conv,
                                 (cache_kv_w128, cache_kv_w512, cache_kv_w2048), w, bb=Bd, ts=T, tm=Bd * T)
    return (y_p, y_s, conv_p, conv_s, bufs_p[0], bufs_p[1], bufs_p[2], bufs_s[0], bufs_s[1], bufs_s[2])
```

```python
import functools

import jax
import jax.numpy as jnp
from jax import lax
from jax.experimental import pallas as pl
from jax.experimental.pallas import tpu as pltpu

D_MODEL = 1024
DEPTH = 4
PAST_LEN = 8192
N_A = DEPTH // 2
N_B = DEPTH - N_A
CONV_WIDTH = 31
CONV_HALO = 32
SUBLANES = 8
LANES = 128
N_HEADS = 16
HEAD_DIM = D_MODEL // N_HEADS
WINDOWS = (128, 512, 2048)
DILATIONS = (1, 4, 16)
N_GROUPS = len(WINDOWS)
BLK = 128
D_FF = (8 * D_MODEL + 3 * 256 - 1) // (3 * 256) * 256
ROPE_THETA = 10000.0
EPS = 1e-6
SCALE = HEAD_DIM ** -0.5
MXU_N = 256
PROJ_SLAB = 2 * MXU_N
VMEM_LIMIT = 56 * 1024 * 1024

F32 = jnp.float32
BF16 = jnp.bfloat16


def _params(*sem):
    return pltpu.CompilerParams(dimension_semantics=sem, vmem_limit_bytes=VMEM_LIMIT)


def _rms(x, g):
    return x * lax.rsqrt(jnp.mean(x * x, axis=-1, keepdims=True) + EPS) * g


def _sigmoid(x):
    return 1.0 / (1.0 + jnp.exp(-x))


def _conv_kernel(x_ref, prev_ref, ng_ref, win_ref, bin_ref, wdw_ref, bdw_ref, lng_ref, lnb_ref,
                 wout_ref, bout_ref, y_ref, st_ref, ext_ref, *, ts):
    s = pl.program_id(1)
    bb = x_ref.shape[0]

    @pl.when(s == 0)
    def _():
        ext_ref[:, 0:CONV_HALO, :] = prev_ref[...]

    @pl.when(s > 0)
    def _():
        ext_ref[:, 0:CONV_HALO, :] = ext_ref[:, ts:ts + CONV_HALO, :]

    x = x_ref[...].reshape(bb * ts, D_MODEL)
    u = _rms(x, ng_ref[...]).astype(BF16)
    a = jnp.dot(u, win_ref[...], preferred_element_type=F32) + bin_ref[...]
    glu = a[:, :D_MODEL] * _sigmoid(a[:, D_MODEL:])
    ext_ref[:, CONV_HALO:CONV_HALO + ts, :] = glu.reshape(bb, ts, D_MODEL)

    off = CONV_HALO - (CONV_WIDTH - 1)
    c = None
    for b in range(SUBLANES):
        rows = ts if b == 0 else ts + SUBLANES
        yb = None
        for a_ in range((off + CONV_WIDTH - 1) // SUBLANES + 1):
            j = SUBLANES * a_ + b - off
            if 0 <= j < CONV_WIDTH:
                term = ext_ref[:, SUBLANES * a_:SUBLANES * a_ + rows, :] * wdw_ref[j:j + 1, :]
                yb = term if yb is None else yb + term
        yb = yb[:, b:b + ts, :]
        c = yb if c is None else c + yb
    c = c.reshape(bb * ts, D_MODEL) + bdw_ref[...]
    mu = jnp.mean(c, axis=-1, keepdims=True)
    cc = c - mu
    yn = cc * lax.rsqrt(jnp.mean(cc * cc, axis=-1, keepdims=True) + EPS) * lng_ref[...] + lnb_ref[...]
    h = (yn * _sigmoid(yn)).astype(BF16)
    out = x + jnp.dot(h, wout_ref[...], preferred_element_type=F32) + bout_ref[...]
    y_ref[...] = out.reshape(bb, ts, D_MODEL)
    st_ref[...] = ext_ref[:, ts:ts + CONV_HALO, :]


def _conv_layer(x, prev32, ng, win, bin_, wdw, bdw, lng, lnb, wout, bout, *, bb, ts):
    B, S, _ = x.shape
    full = lambda shape: pl.BlockSpec(shape, lambda b, s: (0,) * len(shape))
    return pl.pallas_call(
        functools.partial(_conv_kernel, ts=ts),
        grid=(B // bb, S // ts),
        in_specs=[
            pl.BlockSpec((bb, ts, D_MODEL), lambda b, s: (b, s, 0)),
            pl.BlockSpec((bb, CONV_HALO, D_MODEL), lambda b, s: (b, 0, 0)),
            full((1, D_MODEL)), full((D_MODEL, 2 * D_MODEL)), full((1, 2 * D_MODEL)),
            full((CONV_WIDTH, D_MODEL)), full((1, D_MODEL)), full((1, D_MODEL)), full((1, D_MODEL)),
            full((D_MODEL, D_MODEL)), full((1, D_MODEL)),
        ],
        out_specs=[
            pl.BlockSpec((bb, ts, D_MODEL), lambda b, s: (b, s, 0)),
            pl.BlockSpec((bb, CONV_HALO, D_MODEL), lambda b, s: (b, 0, 0)),
        ],
        out_shape=[jax.ShapeDtypeStruct((B, S, D_MODEL), F32),
                   jax.ShapeDtypeStruct((B, CONV_HALO, D_MODEL), F32)],
        scratch_shapes=[pltpu.VMEM((bb, CONV_HALO + ts, D_MODEL), F32)],
        compiler_params=_params("parallel", "arbitrary"),
        name="conv_module",
    )(x, prev32, ng, win, bin_, wdw, bdw, lng, lnb, wout, bout)


FF_SLAB = 256
FFN_ROWS = 1024


def _ffn_kernel(x_ref, g_ref, wgu_ref, wd_ref, y_ref, h_ref):
    x = x_ref[...]
    xn = _rms(x, g_ref[...]).astype(BF16)
    for k in range(D_FF // FF_SLAB):
        gate = jnp.dot(xn, wgu_ref[:, k * FF_SLAB:(k + 1) * FF_SLAB], preferred_element_type=F32)
        up = jnp.dot(xn, wgu_ref[:, D_FF + k * FF_SLAB:D_FF + (k + 1) * FF_SLAB], preferred_element_type=F32)
        h_ref[:, k * FF_SLAB:(k + 1) * FF_SLAB] = (gate * _sigmoid(gate) * up).astype(BF16)
    y_ref[...] = x + jnp.dot(h_ref[...], wd_ref[...], preferred_element_type=F32)


def _ffn_layer(x, g, wgu, wd, *, tm):
    M = x.shape[0]
    resident = lambda shape: pl.BlockSpec(shape, lambda i: (0, 0), pipeline_mode=pl.Buffered(1))
    return pl.pallas_call(
        _ffn_kernel,
        grid=(M // tm,),
        in_specs=[
            pl.BlockSpec((tm, D_MODEL), lambda i: (i, 0)),
            pl.BlockSpec((1, D_MODEL), lambda i: (0, 0)),
            resident((D_MODEL, 2 * D_FF)),
            resident((D_FF, D_MODEL)),
        ],
        out_specs=pl.BlockSpec((tm, D_MODEL), lambda i: (i, 0)),
        out_shape=jax.ShapeDtypeStruct((M, D_MODEL), F32),
        scratch_shapes=[pltpu.VMEM((tm, D_FF), BF16)],
        compiler_params=_params("parallel"),
        name="swiglu_ffn",
    )(x, g, wgu, wd)


def _normed_slab(y, gain, cos, sin, e):
    ssq = jnp.dot((y * y).astype(BF16), e, preferred_element_type=F32)
    yn = y * lax.rsqrt(ssq * (1.0 / HEAD_DIM) + EPS) * gain
    lane = lax.broadcasted_iota(jnp.int32, (y.shape[0], LANES), 1)
    lo_half = (lane % HEAD_DIM) < (HEAD_DIM // 2)
    outs = []
    for k in range(MXU_N // LANES):
        z = yn[:, k * LANES:(k + 1) * LANES]
        partner = jnp.where(lo_half, pltpu.roll(z, LANES - HEAD_DIM // 2, 1), pltpu.roll(z, HEAD_DIM // 2, 1))
        outs.append(z * cos + partner * sin)
    return jnp.concatenate(outs, axis=-1)


def _in_order(pt, v):
    if v.dtype == BF16:
        return jnp.dot(pt, v, preferred_element_type=F32)
    hi = v.astype(BF16)
    rest = v - hi.astype(F32)
    mid = rest.astype(BF16)
    lo = (rest - mid.astype(F32)).astype(BF16)
    return (jnp.dot(pt, hi, preferred_element_type=F32) + jnp.dot(pt, mid, preferred_element_type=F32)
            + jnp.dot(pt, lo, preferred_element_type=F32))


def _proj_kernel(*refs, tm, n_chunks, cps, plain_every, tails, n_s, rs):
    x_ref, g_ref, w_ref, gain_ref, cos_ref, sin_ref, e_ref = refs[:7]
    perm_of = {g: k for k, g in enumerate(g for g, r in enumerate(rs) if r > 1)}
    n_perm = len(perm_of)
    perm_refs = refs[7:7 + n_perm]
    out_refs = refs[7 + n_perm:-1]
    ob_refs, tail_refs = ((), out_refs) if tails else (out_refs, ())
    xn_ref = refs[-1]
    i = pl.program_id(0)
    c = pl.program_id(1)
    s = i % n_s

    @pl.when(c == 0)
    def _():
        xn = _rms(x_ref[...], g_ref[...]).astype(BF16)
        for g, r in enumerate(rs):
            if r == 1:
                xn_ref[g] = xn
            else:
                xn_ref[g] = jnp.dot(perm_refs[perm_of[g]][...], xn, preferred_element_type=F32).astype(BF16)

    def chunk(g, w0, normed, store):
        xn = xn_ref[g]
        for k in range(D_MODEL // PROJ_SLAB):
            y = jnp.dot(xn, w_ref[:, w0 + k * PROJ_SLAB:w0 + (k + 1) * PROJ_SLAB], preferred_element_type=F32)
            for h in range(PROJ_SLAB // MXU_N):
                sl = slice(k * PROJ_SLAB + h * MXU_N, k * PROJ_SLAB + (h + 1) * MXU_N)
                z = y[:, h * MXU_N:(h + 1) * MXU_N]
                if normed:
                    z = _normed_slab(z, gain_ref[g, :, sl], cos_ref[g], sin_ref[g], e_ref[...])
                store(sl, z)

    def step(st):
        for u in range(cps):
            cs = st * cps + u
            g, part = divmod(cs, max(plain_every, 1))
            if not tails:
                def store(sl, z, ob_ref=ob_refs[cs], r=rs[g]):
                    zb = z.astype(BF16)
                    rows = tm // r
                    for cc in range(r):
                        ob_ref[0, cc, 0, :, sl] = zb[cc * rows:(cc + 1) * rows]
                chunk(g, u * D_MODEL, part == 0, store)
            else:
                lg = tails[g]
                keep = min(lg, tm)
                def store(sl, z, tail_ref=tail_refs[g], col0=part * D_MODEL, keep=keep):
                    tail_ref[0, :, col0 + sl.start:col0 + sl.stop] = z[tm - keep:, :]
                pl.when(s >= n_s - max(lg // tm, 1))(functools.partial(chunk, g, u * D_MODEL, part == 0, store))

    if n_chunks == cps:
        step(0)
    else:
        for st in range(n_chunks // cps):
            pl.when(c == st)(functools.partial(step, st))


def _residue_major(n, r):
    pos = jnp.arange(n, dtype=jnp.int32)
    return (pos % (n // r)) * r + pos // (n // r)


def _proj(x, g, w, gain, cos, sin, e, *, tm, plain_every, cps, rs, seq=None, tails=()):
    M = x.shape[0]
    n_chunks = w.shape[1] // D_MODEL
    gdiv = max(plain_every, 1)
    seq = seq or M
    n_groups = len(rs)
    n_seq_tiles = seq // tm
    n_pos_tiles = cos.shape[0] // tm
    n_s = max(-(-max(tails) // tm), 1) if tails else n_seq_tiles
    skip = n_seq_tiles - n_s
    orders = [_residue_major(tm, r) for r in rs]
    n_of = jnp.arange(tm, dtype=jnp.int32)
    perms = [(o[:, None] == n_of[None, :]).astype(BF16) for o, r in zip(orders, rs) if r > 1]
    tile_order = lambda tab, o: tab.reshape(n_pos_tiles, tm, LANES)[:, o].reshape(n_pos_tiles * tm, LANES)
    cos = jnp.stack([tile_order(cos, o) for o in orders])
    sin = jnp.stack([tile_order(sin, o) for o in orders])
    out_shape, out_specs = [], []
    if not tails:
        for cs in range(n_chunks):
            r = rs[cs // gdiv]
            out_shape.append(jax.ShapeDtypeStruct((M // seq, r, n_s, tm // r, D_MODEL), BF16))
            out_specs.append(pl.BlockSpec((1, r, 1, tm // r, D_MODEL), lambda i, c: (i // n_s, 0, i % n_s, 0, 0)))
    for lg in tails:
        assert all(r == 1 for r in rs)
        out_shape.append(jax.ShapeDtypeStruct((M // seq, lg, 2 * D_MODEL), F32))
        first = n_s - max(lg // tm, 1)
        out_specs.append(pl.BlockSpec((1, min(lg, tm), 2 * D_MODEL),
                                      lambda i, c, first=first: (i // n_s, jnp.maximum(i % n_s - first, 0), 0)))
    table = pl.BlockSpec((n_groups, tm, LANES), lambda i, c: (0, (skip + i % n_s) % n_pos_tiles, 0))
    if tails:
        assert cps == gdiv
        firsts = [n_s - max(lg // tm, 1) for lg in tails]
        widest = max(range(n_groups), key=lambda g: tails[g])

        def w_block(i, c):
            first_tile = functools.reduce(lambda acc, g: jnp.where(c == g, firsts[g], acc), range(n_groups), 0)
            return 0, jnp.where(i % n_s >= first_tile, c, widest)
    else:
        w_block = lambda i, c: (0, c)
    outs = pl.pallas_call(
        functools.partial(_proj_kernel, tm=tm, n_chunks=n_chunks, cps=cps, plain_every=plain_every,
                          tails=tuple(tails), n_s=n_s, rs=tuple(rs)),
        grid=(M // seq * n_s, n_chunks // cps),
        in_specs=[
            pl.BlockSpec((tm, D_MODEL), lambda i, c: (i // n_s * n_seq_tiles + skip + i % n_s, 0)),
            pl.BlockSpec((1, D_MODEL), lambda i, c: (0, 0)),
            pl.BlockSpec((D_MODEL, cps * D_MODEL), w_block),
            pl.BlockSpec((n_groups, 1, D_MODEL), lambda i, c: (0, 0, 0)),
            table, table,
            pl.BlockSpec((MXU_N, MXU_N), lambda i, c: (0, 0)),
        ] + [pl.BlockSpec((tm, tm), lambda i, c: (0, 0))] * len(perms),
        out_specs=out_specs,
        out_shape=out_shape,
        scratch_shapes=[pltpu.VMEM((n_groups, tm, D_MODEL), BF16)],
        compiler_params=_params("arbitrary", "arbitrary"),
        name="norm_proj_tails" if tails else "norm_proj_rotary",
    )(x, g, w, gain, cos, sin, e, *perms)
    if tails:
        return list(outs)
    return [o.reshape(M // seq * rs[cs // gdiv], seq // rs[cs // gdiv], D_MODEL) for cs, o in enumerate(outs)]


def _rope_tables(pos):
    half = HEAD_DIM // 2
    inv = ROPE_THETA ** (-jnp.arange(half, dtype=F32) / half)
    ang = pos.astype(F32)[:, None] * inv[None, :]
    cos, sin = jnp.cos(ang), jnp.sin(ang)
    reps = LANES // HEAD_DIM
    return jnp.tile(cos, (1, 2 * reps)), jnp.tile(jnp.concatenate([-sin, sin], axis=1), (1, reps))


def _qk(q, k):
    return lax.dot_general(q, k, (((1,), (1,)), ((), ())), preferred_element_type=F32)


def _head_masks(rows):
    first = lax.broadcasted_iota(jnp.int32, (rows, LANES), 1) < HEAD_DIM
    return first


def _band_attn_kernel(q_ref, k_ref, v_ref, o_ref, lse_ref, kf_ref, vf_ref, s_ref, p_ref, *, tq):
    n = pl.program_id(2)
    rows = o_ref.shape[3]

    def store(ref, i, sl, val):
        if rows >= BLK:
            per = rows // BLK
            ref[0, i // per, 0, pl.ds(pl.multiple_of((i % per) * BLK, BLK), BLK), sl] = val
        else:
            for u in range(BLK // rows):
                ref[0, i * (BLK // rows) + u, 0, :, sl] = val[u * rows:(u + 1) * rows]

    @pl.when(n == 0)
    def _():
        kf_ref[0:BLK, :] = jnp.zeros((BLK, D_MODEL), BF16)
        vf_ref[0:BLK, :] = jnp.zeros((BLK, D_MODEL), BF16)

    @pl.when(n > 0)
    def _():
        kf_ref[0:BLK, :] = kf_ref[tq:tq + BLK, :]
        vf_ref[0:BLK, :] = vf_ref[tq:tq + BLK, :]

    kf_ref[BLK:BLK + tq, :] = k_ref[0]
    vf_ref[BLK:BLK + tq, :] = v_ref[0]

    qi = lax.broadcasted_iota(jnp.int32, (BLK, 2 * BLK), 0)
    ki = lax.broadcasted_iota(jnp.int32, (BLK, 2 * BLK), 1)
    dist = BLK + qi - ki
    band = (dist >= 0) & (dist <= BLK)
    first = _head_masks(BLK)
    lane = lax.broadcasted_iota(jnp.int32, (BLK, LANES), 1)
    zero = jnp.zeros((BLK, LANES), BF16)

    def block(i, carry):
        r0 = pl.multiple_of(i * BLK, BLK)
        mask = band & ((ki >= BLK) | (n + i > 0))
        for p in range(N_HEADS // 2):
            sl = slice(p * LANES, (p + 1) * LANES)
            qp = q_ref[0, pl.ds(r0, BLK), sl]
            kp = kf_ref[pl.ds(r0, 2 * BLK), sl]
            s_ref[2 * p] = jnp.where(mask, _qk(jnp.where(first, qp, zero), kp), -jnp.inf)
            s_ref[2 * p + 1] = jnp.where(mask, _qk(jnp.where(first, zero, qp), kp), -jnp.inf)
        s = s_ref[...]
        m = jnp.max(s, axis=-1, keepdims=True)
        e = jnp.exp(s - m)
        l = jnp.sum(e, axis=-1, keepdims=True)
        p_ref[...] = e.astype(BF16)
        inv = 1.0 / l
        lse = m + jnp.log(l)
        lse_tile = jnp.zeros((BLK, LANES), F32)
        for p in range(N_HEADS // 2):
            sl = slice(p * LANES, (p + 1) * LANES)
            vp = vf_ref[pl.ds(r0, 2 * BLK), sl]
            oa = jnp.dot(p_ref[2 * p], vp, preferred_element_type=F32) * inv[2 * p]
            ob = jnp.dot(p_ref[2 * p + 1], vp, preferred_element_type=F32) * inv[2 * p + 1]
            store(o_ref, i, sl, jnp.where(first, oa, ob).astype(BF16))
            lse_tile = jnp.where(lane == 2 * p, lse[2 * p], lse_tile)
            lse_tile = jnp.where(lane == 2 * p + 1, lse[2 * p + 1], lse_tile)
        store(lse_ref, i, slice(None), lse_tile)
        return carry

    lax.fori_loop(0, tq // BLK, block, 0)


def _band_attention(q, k, v, g, *, B, S, tm):
    r = DILATIONS[g]
    L = S // r
    tq = min(L, 4 * BLK)
    rows = tm // r
    nt = tq // rows
    spec = pl.BlockSpec((1, tq, D_MODEL), lambda b, c, n: (b * r + c, n, 0))
    o, lse = pl.pallas_call(
        functools.partial(_band_attn_kernel, tq=tq),
        grid=(B, r, L // tq),
        in_specs=[spec, spec, spec],
        out_specs=[
            pl.BlockSpec((1, nt, 1, rows, D_MODEL), lambda b, c, n: (b, n, c, 0, 0)),
            pl.BlockSpec((1, nt, 1, rows, LANES), lambda b, c, n: (b, n, c, 0, 0)),
        ],
        out_shape=[jax.ShapeDtypeStruct((B, S // tm, r, rows, D_MODEL), BF16),
                   jax.ShapeDtypeStruct((B, S // tm, r, rows, LANES), F32)],
        scratch_shapes=[pltpu.VMEM((BLK + tq, D_MODEL), BF16), pltpu.VMEM((BLK + tq, D_MODEL), BF16),
                        pltpu.VMEM((N_HEADS, BLK, 2 * BLK), F32), pltpu.VMEM((N_HEADS, BLK, 2 * BLK), BF16)],
        compiler_params=_params("parallel", "parallel", "arbitrary"),
        name=f"band_attention_g{g}",
    )(q, k, v)
    return o.reshape(B * S // tm, r, rows, D_MODEL), lse.reshape(B * S // tm, r, rows, LANES)


NEW_PAD = LANES


def _cache_attn_kernel(*refs, lb, r, t, hb, roll):
    if roll:
        q_ref, kn_ref, vn_ref, newt_ref, c_ref, o_ref, lse_ref, roll_ref, new_ref = refs
    else:
        q_ref, kn_ref, vn_ref, c_ref, o_ref, lse_ref, new_ref = refs
    j = pl.program_id(1)
    cols = hb * HEAD_DIM
    new_ref[...] = jnp.zeros((NEW_PAD, 2 * cols), BF16)
    new_ref[0:t, 0:cols] = kn_ref[0]
    new_ref[0:t, cols:2 * cols] = vn_ref[0]

    ti = lax.broadcasted_iota(jnp.int32, (t, NEW_PAD + lb), 0)
    rho = lax.broadcasted_iota(jnp.int32, (t, NEW_PAD + lb), 1)
    idx = jnp.where(rho < NEW_PAD, lb + rho, rho - NEW_PAD)
    d = lb + ti - idx
    mask = ((rho < t) | (rho >= NEW_PAD)) & (d >= 0) & (d <= BLK * r) & ((d & (r - 1)) == 0)
    first = _head_masks(t)
    lane = lax.broadcasted_iota(jnp.int32, (t, LANES), 1)
    zero = jnp.zeros((t, LANES), BF16)

    lse_tile = jnp.zeros((t, LANES), F32)
    for p in range(hb // 2):
        ks = slice(p * LANES, (p + 1) * LANES)
        vs = slice(cols + p * LANES, cols + (p + 1) * LANES)
        qp = q_ref[0, :, ks]
        k2 = c_ref[0, 0, 2 * p:2 * p + 2].reshape(LANES, lb).astype(BF16)
        v2 = c_ref[0, 1, 2 * p:2 * p + 2].reshape(LANES, lb).astype(BF16)
        outs = []
        for hh, qm in enumerate((jnp.where(first, qp, zero), jnp.where(first, zero, qp))):
            s = jnp.concatenate([_qk(qm, new_ref[:, ks]), jnp.dot(qm, k2, preferred_element_type=F32)], axis=1)
            s = jnp.where(mask, s, -jnp.inf)
            m = jnp.max(s, axis=-1, keepdims=True)
            e = jnp.exp(s - m)
            l = jnp.sum(e, axis=-1, keepdims=True)
            eb = e.astype(BF16)
            o = (jnp.dot(eb[:, :NEW_PAD], new_ref[:, vs], preferred_element_type=F32)
                 + _qk(eb[:, NEW_PAD:], v2)) * (1.0 / l)
            outs.append(o)
            lse_tile = jnp.where(lane == j * hb + 2 * p + hh, m + jnp.log(l), lse_tile)
        o_ref[0, :, ks] = jnp.where(first, outs[0], outs[1])
    lse_ref[0, 0] = lse_tile

    if roll:
        for kv in range(2):
            for h in range(hb):
                roll_ref[0, kv, h] = pltpu.roll(c_ref[0, kv, h], lb - t, 1)
                roll_ref[0, kv, h, :, lb - t:lb] = newt_ref[0, kv, h]


CACHE_BLOCK_BYTES = 8 * 1024 * 1024


def _cache_attention(q, kn, vn, cache_t, g, new_t=None):
    B, T, _ = q.shape
    lb = cache_t.shape[-1]
    r = DILATIONS[g]
    hb = N_HEADS
    while 2 * hb * HEAD_DIM * lb * 4 > CACHE_BLOCK_BYTES:
        hb //= 2
    nj = N_HEADS // hb
    roll = new_t is not None
    new = pl.BlockSpec((1, T, hb * HEAD_DIM), lambda b, j: (b, 0, j))
    heads = lambda w: pl.BlockSpec((1, 2, hb, HEAD_DIM, w), lambda b, j: (b, 0, j, 0, 0))
    out_specs = [new, pl.BlockSpec((1, 1, T, LANES), lambda b, j: (b, j, 0, 0))]
    out_shape = [jax.ShapeDtypeStruct((B, T, D_MODEL), F32), jax.ShapeDtypeStruct((B, nj, T, LANES), F32)]
    if roll:
        out_specs.append(heads(lb))
        out_shape.append(jax.ShapeDtypeStruct(cache_t.shape, cache_t.dtype))
    o, lse, *rolled = pl.pallas_call(
        functools.partial(_cache_attn_kernel, lb=lb, r=r, t=T, hb=hb, roll=roll),
        grid=(B, nj),
        in_specs=[new, new, new] + ([heads(T)] if roll else []) + [heads(lb)],
        out_specs=out_specs,
        out_shape=out_shape,
        scratch_shapes=[pltpu.VMEM((NEW_PAD, 2 * hb * HEAD_DIM), BF16)],
        compiler_params=_params("parallel", "parallel"),
        name=f"cache_attention_g{g}",
    )(q, kn, vn, *([new_t] if roll else []), cache_t)
    return o.reshape(B * T, D_MODEL), jnp.sum(lse, axis=1).reshape(B * T, LANES), rolled


MERGE_SPLIT = 2


def _merge_kernel(*refs, rs):
    n = len(rs)
    x_ref = refs[0]
    o_refs = refs[1:1 + n]
    l_refs = refs[1 + n:1 + 2 * n]
    pt_refs = refs[1 + 2 * n:-2]
    wo_ref, y_ref = refs[-2:]
    tm = x_ref.shape[0]
    rb = tm // MERGE_SPLIT
    first = _head_masks(rb)
    for rows in (slice(b * rb, (b + 1) * rb) for b in range(MERGE_SPLIT)):
        os_, ls = [], []
        k = 0
        for o_ref, l_ref, r in zip(o_refs, l_refs, rs):
            if r > 1:
                pt = pt_refs[k][rows, :]
                o = _in_order(pt, o_ref[0].reshape(tm, D_MODEL))
                l = _in_order(pt, l_ref[0].reshape(tm, LANES))
                k += 1
            else:
                o, l = o_ref[0, 0, rows, :], l_ref[0, 0, rows, :]
            os_.append(o)
            ls.append(l)
        m = functools.reduce(jnp.maximum, ls)
        es = [jnp.exp(l - m) for l in ls]
        inv = 1.0 / functools.reduce(lambda a, b: a + b, es)
        ws = [e * inv for e in es]
        cols = []
        for p in range(N_HEADS // 2):
            sl = slice(p * LANES, (p + 1) * LANES)
            acc = None
            for w, o in zip(ws, os_):
                wp = jnp.where(first, w[:, 2 * p:2 * p + 1], w[:, 2 * p + 1:2 * p + 2])
                term = wp * o[:, sl].astype(F32)
                acc = term if acc is None else acc + term
            cols.append(acc.astype(BF16))
        merged = jnp.concatenate(cols, axis=-1)
        y_ref[rows, :] = x_ref[rows, :] + jnp.dot(merged, wo_ref[...], preferred_element_type=F32)


def _merge_layer(x, outs, lses, wo, *, tm, rs):
    M = x.shape[0]
    row = lambda w: pl.BlockSpec((tm, w), lambda i: (i, 0))
    part = lambda r, w: pl.BlockSpec((1, r, tm // r, w), lambda i: (i, 0, 0, 0))
    n_of = jnp.arange(tm, dtype=jnp.int32)
    pts = [(n_of[:, None] == _residue_major(tm, r)[None, :]).astype(BF16) for r in rs if r > 1]
    return pl.pallas_call(
        functools.partial(_merge_kernel, rs=tuple(rs)),
        grid=(M // tm,),
        in_specs=([row(D_MODEL)] + [part(r, D_MODEL) for r in rs] + [part(r, LANES) for r in rs]
                  + [pl.BlockSpec((tm, tm), lambda i: (0, 0))] * len(pts)
                  + [pl.BlockSpec((D_MODEL, D_MODEL), lambda i: (0, 0))]),
        out_specs=row(D_MODEL),
        out_shape=jax.ShapeDtypeStruct((M, D_MODEL), F32),
        compiler_params=_params("parallel"),
        name="merge_groups_wo",
    )(x, *outs, *lses, *pts, wo)


def _seq_minor(a):
    return jnp.transpose(a, (0, 2, 3, 4, 1))


def _seq_major(a):
    return jnp.transpose(a, (0, 4, 1, 2, 3))


def _trunk(x, pos, conv_prev, kv_caches, w, *, bb, ts, tm):
    B, S, _ = x.shape
    M = B * S
    cos, sin = _rope_tables(pos)
    if S < tm:
        cos, sin = jnp.tile(cos, (tm // S, 1)), jnp.tile(sin, (tm // S, 1))
    prev32 = jnp.pad(conv_prev, ((0, 0), (0, 0), (CONV_HALO - (CONV_WIDTH - 1), 0), (0, 0)))

    rs = DILATIONS if kv_caches is None else (1,) * N_GROUPS
    seq = S if kv_caches is None else M
    conv_states = []
    new_bufs = []
    kv = None
    for layer in range(DEPTH):
        if layer < N_A:
            x, st = _conv_layer(x, prev32[layer], w['a_norm'][layer], w['a_w_in'][layer], w['a_b_in'][layer],
                                w['a_w_dw'][layer], w['a_b_dw'][layer], w['a_ln_g'][layer], w['a_ln_b'][layer],
                                w['a_w_out'][layer], w['a_b_out'][layer], bb=bb, ts=ts)
            conv_states.append(st[:, CONV_HALO - (CONV_WIDTH - 1):])
            xf = x.reshape(M, D_MODEL)
        else:
            j = layer - N_A
            if layer == N_A:
                tails = tuple(min(wd, S) for wd in WINDOWS) if kv_caches is None else (M,) * N_GROUPS
                kv_args = (xf, w['kv_norm'], w['w_kv'], w['k_gain'], cos, sin, w['head_ones'])
                kv = _proj(*kv_args, tm=tm, plain_every=2, cps=2, rs=rs, seq=seq)
                tails_f32 = [t.reshape(B, -1, 2, N_HEADS, HEAD_DIM) for t in
                             _proj(*kv_args, tm=tm, plain_every=2, cps=2, rs=(1,) * N_GROUPS, seq=seq, tails=tails)]
                if kv_caches is None:
                    new_bufs = tails_f32
            q = _proj(xf, w['b_norm'][j], w['w_q'][j], w['q_gain'][j], cos, sin, w['head_ones'],
                      tm=tm, plain_every=0, cps=N_GROUPS, rs=rs, seq=seq)
            outs, lses = [], []
            for g in range(N_GROUPS):
                if kv_caches is None:
                    o, l = _band_attention(q[g], kv[2 * g], kv[2 * g + 1], g, B=B, S=S, tm=tm)
                else:
                    new = lambda a: a.reshape(B, S, D_MODEL)
                    new_t = _seq_minor(tails_f32[g]) if layer == N_A else None
                    o, l, rolled = _cache_attention(new(q[g]), new(kv[2 * g]), new(kv[2 * g + 1]),
                                                    _seq_minor(kv_caches[g]), g, new_t)
                    if rolled:
                        new_bufs.append(_seq_major(rolled[0]))
                    o, l = o.reshape(M // tm, 1, tm, D_MODEL), l.reshape(M // tm, 1, tm, LANES)
                outs.append(o)
                lses.append(l)
            xf = _merge_layer(xf, outs, lses, w['w_o'][j], tm=tm, rs=rs)
        xf = _ffn_layer(xf, w['ffn_norm'][layer], w['w_gate_up'][layer], w['w_down'][layer],
                        tm=min(M, FFN_ROWS))
        x = xf.reshape(B, S, D_MODEL)
    return x, jnp.stack(conv_states, axis=0), new_bufs


def kernel(x_prompt, x_sample, cache_conv, cache_kv_w128, cache_kv_w512, cache_kv_w2048,
           a_norm, a_w_in, a_b_in, a_w_dw, a_b_dw, a_ln_g, a_ln_b, a_w_out, a_b_out,
           kv_norm, w_kv, k_norm, b_norm, w_q, q_norm, w_o, ffn_norm, w_gate_up, w_down):
    B, S, _ = x_prompt.shape
    Bd, T, _ = x_sample.shape
    row = lambda a: a.reshape(a.shape[:-1] + (1, a.shape[-1]))
    head_of = jnp.arange(MXU_N, dtype=jnp.int32) // HEAD_DIM
    w = dict(
        a_norm=row(a_norm), a_w_in=a_w_in.astype(BF16), a_b_in=row(a_b_in), a_w_dw=a_w_dw, a_b_dw=row(a_b_dw),
        a_ln_g=row(a_ln_g), a_ln_b=row(a_ln_b), a_w_out=a_w_out.astype(BF16), a_b_out=row(a_b_out),
        kv_norm=row(kv_norm), w_kv=w_kv.astype(BF16), b_norm=row(b_norm), w_q=w_q.astype(BF16),
        w_o=w_o.astype(BF16), ffn_norm=row(ffn_norm), w_gate_up=w_gate_up.astype(BF16), w_down=w_down.astype(BF16),
        k_gain=row(jnp.tile(k_norm, (1, N_HEADS))),
        q_gain=row(jnp.tile(q_norm, (1, 1, N_HEADS)) * SCALE),
        head_ones=(head_of[:, None] == head_of[None, :]).astype(BF16),
    )
    pos_prompt = jnp.arange(S, dtype=jnp.int32)
    pos_sample = PAST_LEN + jnp.arange(T, dtype=jnp.int32)
    conv_zero = jnp.zeros((N_A, B, CONV_WIDTH - 1, D_MODEL), x_prompt.dtype)
    y_p, conv_p, bufs_p = _trunk(x_prompt, pos_prompt, conv_zero, None, w, bb=1, ts=512, tm=512)
    y_s, conv_s, bufs_s = _trunk(x_sample, pos_sample, cache_conv,
                                 (cache_kv_w128, cache_kv_w512, cache_kv_w2048), w, bb=Bd, ts=T, tm=Bd * T)
    return (y_p, y_s, conv_p, conv_s, bufs_p[0], bufs_p[1], bufs_p[2], bufs_s[0], bufs_s[1], bufs_s[2])
```
